```python
import math
import jax
import jax.numpy as jnp
from jax import lax
import numpy as np

D_MODEL = 2048
BATCH = 4
SEQ = 4096
DEPTH = 4

N_A_LAYERS = DEPTH // 2
N_B_LAYERS = DEPTH - N_A_LAYERS
HEAD_DIM = 128
ROT_DIM = HEAD_DIM // 4
ROPE_THETA = 500000.0
NORM_EPS = 1e-6
NEG_INF = -1e30
A_PATTERNS = ((128, 1), (512, 4), (2048, 16))
N_A_GROUPS = len(A_PATTERNS)
A_HEADS = 8
A_GROUP_WIDTH = A_HEADS * HEAD_DIM
A_QKV_WIDTH = N_A_GROUPS * 3 * A_GROUP_WIDTH
BAND_BLOCK = 128
B_HEADS = 8
B_QK_WIDTH = B_HEADS * 2 * HEAD_DIM
B_V_DIM = 2 * HEAD_DIM
B_V_WIDTH = B_HEADS * B_V_DIM
Q_BLOCK = 128
D_FF = 4 * D_MODEL
N_MOD = 6

kernel_name = "yoco_dilated_diffattn_adaln_trunk"


def rms_norm(t, gain):
    t32 = t.astype(jnp.float32)
    y = t32 * lax.rsqrt(jnp.mean(t32 * t32, axis=-1, keepdims=True) + NORM_EPS)
    return (y * gain.astype(jnp.float32)).astype(t.dtype)


def modulate(h, shift, scale):
    return h * (1.0 + scale[:, None, :]) + shift[:, None, :]


def rope_tables(positions):
    inv_freq = ROPE_THETA ** (-jnp.arange(0, ROT_DIM, 2, dtype=jnp.float32) / ROT_DIM)
    ang = positions.astype(jnp.float32)[..., None] * inv_freq
    return jnp.cos(ang)[:, :, None, :], jnp.sin(ang)[:, :, None, :]


def apply_rope(t, cos, sin):
    half = ROT_DIM // 2
    cos = cos.astype(t.dtype)
    sin = sin.astype(t.dtype)
    t1 = t[..., :half]
    t2 = t[..., half:ROT_DIM]
    return jnp.concatenate([t1 * cos - t2 * sin, t2 * cos + t1 * sin, t[..., ROT_DIM:]], axis=-1)


def rope_pairs(t, cos, sin):
    b, s, h, two, e = t.shape
    return apply_rope(t.reshape(b, s, h * two, e), cos, sin).reshape(b, s, h, two, e)


def dilated_window_attention(q, k, v, window, dilation):
    bsz, seq, heads, hd = q.shape
    sub_len = seq // dilation
    sub_win = window // dilation
    n_blk = -(-sub_len // BAND_BLOCK)
    pad = n_blk * BAND_BLOCK - sub_len

    def to_blocks(t):
        e = t.shape[-1]
        t = t.reshape(bsz, sub_len, dilation, heads, e).transpose(0, 2, 1, 3, 4)
        t = jnp.pad(t, ((0, 0), (0, 0), (0, pad), (0, 0), (0, 0)))
        return t.reshape(bsz, dilation, n_blk, BAND_BLOCK, heads, e)

    def with_prev(t):
        prev = jnp.pad(t[:, :, :-1], ((0, 0), (0, 0), (1, 0), (0, 0), (0, 0), (0, 0)))
        return jnp.concatenate([prev, t], axis=3)

    def from_blocks(t):
        e = t.shape[-1]
        t = t.reshape(bsz, dilation, n_blk * BAND_BLOCK, heads, e)[:, :, :sub_len]
        return t.transpose(0, 2, 1, 3, 4).reshape(bsz, seq, heads, e)

    qb = to_blocks(q)
    kb = with_prev(to_blocks(k))
    vb = with_prev(to_blocks(v))
    s = jnp.einsum('brnqhe,brnkhe->brnhqk', qb, kb,
                   preferred_element_type=jnp.float32) * (hd ** -0.5)
    qi = jnp.arange(BAND_BLOCK)[:, None]
    ki = jnp.arange(2 * BAND_BLOCK)[None, :]
    dist = BAND_BLOCK + qi - ki
    kpos = jnp.arange(n_blk)[:, None, None] * BAND_BLOCK - BAND_BLOCK + ki[None]
    valid = (dist >= 0) & (dist <= sub_win) & (kpos >= 0)
    s = jnp.where(valid[:, None], s, NEG_INF)
    m = jnp.max(s, axis=-1, keepdims=True)
    p = jnp.exp(s - m)
    den = jnp.sum(p, axis=-1, keepdims=True)
    o = jnp.einsum('brnhqk,brnkhe->brnqhe', p.astype(v.dtype), vb,
                   preferred_element_type=jnp.float32)
    o = o / jnp.swapaxes(den, 3, 4)
    lse = jnp.swapaxes(m + jnp.log(den), 3, 4)
    return from_blocks(o), from_blocks(lse)[..., 0]


def mixer_a(h, cos, sin, w_qkv, q_gain, k_gain, w_o):
    bsz, seq, _ = h.shape
    qkv = (h @ w_qkv).reshape(bsz, seq, N_A_GROUPS, 3, A_HEADS, HEAD_DIM)
    outs, lses = [], []
    for g, (window, dilation) in enumerate(A_PATTERNS):
        q = apply_rope(rms_norm(qkv[:, :, g, 0], q_gain[g]), cos, sin)
        k = apply_rope(rms_norm(qkv[:, :, g, 1], k_gain[g]), cos, sin)
        o, lse = dilated_window_attention(q, k, qkv[:, :, g, 2], window, dilation)
        outs.append(o)
        lses.append(lse)
    alpha = jax.nn.softmax(jnp.stack(lses), axis=0)
    o = jnp.sum(alpha[..., None] * jnp.stack(outs), axis=0)
    return o.reshape(bsz, seq, A_GROUP_WIDTH).astype(h.dtype) @ w_o


def shared_kv(x, c_act, w_ada, b_ada, g, w_k, w_v, k_gain, cos, sin):
    bsz, seq, _ = x.shape
    shift, scale = jnp.split(c_act @ w_ada + b_ada, 2, axis=-1)
    h = modulate(rms_norm(x, g), shift, scale)
    k = rms_norm((h @ w_k).reshape(bsz, seq, B_HEADS, 2, HEAD_DIM), k_gain)
    k = rope_pairs(k, cos, sin)
    v = (h @ w_v).reshape(bsz, seq, B_HEADS, B_V_DIM)
    return k, v


def differential_attention(q, k, v, lam):
    bsz, seq, heads, _, hd = q.shape
    n_blk = seq // Q_BLOCK
    qb = jnp.swapaxes(q.reshape(bsz, n_blk, Q_BLOCK, heads, 2, hd), 0, 1)
    kpos = jnp.arange(seq)

    def block(args):
        q_blk, i = args
        s = jnp.einsum('bqhce,bkhce->bhcqk', q_blk, k,
                       preferred_element_type=jnp.float32) * (hd ** -0.5)
        qpos = i * Q_BLOCK + jnp.arange(Q_BLOCK)
        s = jnp.where(kpos[None, :] <= qpos[:, None], s, NEG_INF)
        p = jax.nn.softmax(s, axis=-1)
        a = p[:, :, 0] - lam * p[:, :, 1]
        return jnp.einsum('bhqk,bkhe->bqhe', a.astype(v.dtype), v,
                          preferred_element_type=jnp.float32)

    o = lax.map(block, (qb, jnp.arange(n_blk)))
    return jnp.swapaxes(o, 0, 1).reshape(bsz, seq, heads, 2 * hd)


def mixer_b(h, k, v, cos, sin, w_q, q_gain, lam_params, subln_g, w_o, lam_init):
    bsz, seq, _ = h.shape
    q = rope_pairs(rms_norm((h @ w_q).reshape(bsz, seq, B_HEADS, 2, HEAD_DIM), q_gain), cos, sin)
    lp = lam_params.astype(jnp.float32)
    lam = jnp.exp(jnp.sum(lp[0] * lp[1])) - jnp.exp(jnp.sum(lp[2] * lp[3])) + lam_init
    o = differential_attention(q, k, v, lam)
    o = rms_norm(o, subln_g) * (1.0 - lam_init)
    return o.reshape(bsz, seq, B_V_WIDTH).astype(h.dtype) @ w_o


def sq_relu_mlp(h, w1, w2):
    return jnp.square(jax.nn.relu(h @ w1)) @ w2


def setup_inputs(seed: int = 0) -> dict:
    key = jax.random.key(seed)
    ks = jax.random.split(key, 24)
    f32 = jnp.float32
    nrm = lambda k, shape, std: jax.random.normal(k, shape, f32) * std
    ada_std = 0.5 * D_MODEL ** -0.5
    x = nrm(ks[0], (BATCH, SEQ, D_MODEL), 1.0)
    c = nrm(ks[1], (BATCH, D_MODEL), 1.0)
    positions = (jnp.arange(SEQ, dtype=jnp.int32)[None, :]
                 + jax.random.randint(ks[2], (BATCH, 1), 0, 4096, dtype=jnp.int32))
    ada_w = nrm(ks[3], (DEPTH, D_MODEL, N_MOD * D_MODEL), ada_std)
    ada_b = nrm(ks[4], (DEPTH, N_MOD * D_MODEL), 0.01)
    norm_g = 1.0 + nrm(ks[5], (DEPTH, 2, D_MODEL), 0.02)
    a_w_qkv = nrm(ks[6], (N_A_LAYERS, D_MODEL, A_QKV_WIDTH), D_MODEL ** -0.5)
    a_q_gain = 1.0 + nrm(ks[7], (N_A_LAYERS, N_A_GROUPS, HEAD_DIM), 0.02)
    a_k_gain = 1.0 + nrm(ks[8], (N_A_LAYERS, N_A_GROUPS, HEAD_DIM), 0.02)
    a_w_o = nrm(ks[9], (N_A_LAYERS, A_GROUP_WIDTH, D_MODEL), A_GROUP_WIDTH ** -0.5)
    kv_ada_w = nrm(ks[10], (D_MODEL, 2 * D_MODEL), ada_std)
    kv_ada_b = nrm(ks[11], (2 * D_MODEL,), 0.01)
    kv_norm_g = 1.0 + nrm(ks[12], (D_MODEL,), 0.02)
    kv_w_k = nrm(ks[13], (D_MODEL, B_QK_WIDTH), D_MODEL ** -0.5)
    kv_w_v = nrm(ks[14], (D_MODEL, B_V_WIDTH), D_MODEL ** -0.5)
    kv_k_gain = 1.0 + nrm(ks[15], (2, HEAD_DIM), 0.02)
    b_w_q = nrm(ks[16], (N_B_LAYERS, D_MODEL, B_QK_WIDTH), D_MODEL ** -0.5)
    b_q_gain = 1.0 + nrm(ks[17], (N_B_LAYERS, 2, HEAD_DIM), 0.02)
    b_lambda = nrm(ks[18], (N_B_LAYERS, 4, HEAD_DIM), 0.1)
    b_subln_g = 1.0 + nrm(ks[19], (N_B_LAYERS, B_V_DIM), 0.02)
    b_w_o = nrm(ks[20], (N_B_LAYERS, B_V_WIDTH, D_MODEL), B_V_WIDTH ** -0.5)
    mlp_w1 = nrm(ks[21], (DEPTH, D_MODEL, D_FF), D_MODEL ** -0.5)
    mlp_w2 = nrm(ks[22], (DEPTH, D_FF, D_MODEL), D_FF ** -0.5)
    return {"x": x, "c": c, "positions": positions, "ada_w": ada_w, "ada_b": ada_b,
            "norm_g": norm_g, "a_w_qkv": a_w_qkv, "a_q_gain": a_q_gain, "a_k_gain": a_k_gain,
            "a_w_o": a_w_o, "kv_ada_w": kv_ada_w, "kv_ada_b": kv_ada_b, "kv_norm_g": kv_norm_g,
            "kv_w_k": kv_w_k, "kv_w_v": kv_w_v, "kv_k_gain": kv_k_gain, "b_w_q": b_w_q,
            "b_q_gain": b_q_gain, "b_lambda": b_lambda, "b_subln_g": b_subln_g, "b_w_o": b_w_o,
            "mlp_w1": mlp_w1, "mlp_w2": mlp_w2}


def reference(x, c, positions, ada_w, ada_b, norm_g, a_w_qkv, a_q_gain, a_k_gain, a_w_o,
              kv_ada_w, kv_ada_b, kv_norm_g, kv_w_k, kv_w_v, kv_k_gain, b_w_q, b_q_gain,
              b_lambda, b_subln_g, b_w_o, mlp_w1, mlp_w2):
    c_act = jax.nn.silu(c)
    cos, sin = rope_tables(positions)
    k_sh = v_sh = None
    for layer in range(DEPTH):
        if layer == N_A_LAYERS:
            k_sh, v_sh = shared_kv(x, c_act, kv_ada_w, kv_ada_b, kv_norm_g, kv_w_k, kv_w_v,
                                   kv_k_gain, cos, sin)
        mod = c_act @ ada_w[layer] + ada_b[layer]
        shift_t, scale_t, gate_t, shift_m, scale_m, gate_m = jnp.split(mod, N_MOD, axis=-1)
        h = modulate(rms_norm(x, norm_g[layer, 0]), shift_t, scale_t)
        if layer < N_A_LAYERS:
            y = mixer_a(h, cos, sin, a_w_qkv[layer], a_q_gain[layer], a_k_gain[layer], a_w_o[layer])
        else:
            j = layer - N_A_LAYERS
            lam_init = 0.8 - 0.6 * math.exp(-0.3 * layer)
            y = mixer_b(h, k_sh, v_sh, cos, sin, b_w_q[j], b_q_gain[j], b_lambda[j],
                        b_subln_g[j], b_w_o[j], lam_init)
        x = x + gate_t[:, None, :] * y
        h = modulate(rms_norm(x, norm_g[layer, 1]), shift_m, scale_m)
        x = x + gate_m[:, None, :] * sq_relu_mlp(h, mlp_w1[layer], mlp_w2[layer])
    return x
```

```python
import functools
import math

import jax
import jax.numpy as jnp
from jax import lax
from jax.experimental import pallas as pl
from jax.experimental.pallas import tpu as pltpu

F32 = jnp.float32
BF16 = jnp.bfloat16

HEAD_DIM = 128
ROT_DIM = HEAD_DIM // 4
ROPE_THETA = 500000.0
NORM_EPS = 1e-6
NEG_INF = -1e30
A_PATTERNS = ((128, 1), (512, 4), (2048, 16))
A_HEADS = 8
A_GROUP_WIDTH = A_HEADS * HEAD_DIM
BAND_BLOCK = 128
B_HEADS = 8
B_V_DIM = 2 * HEAD_DIM
N_MOD = 6

V7X_VMEM_LIMIT_BYTES = 56 * 1024 * 1024
LANES = 128


def _params(semantics):
    return pltpu.CompilerParams(dimension_semantics=semantics,
                                vmem_limit_bytes=V7X_VMEM_LIMIT_BYTES)


def _tile(n, target):
    t = min(n, target)
    while n % t:
        t //= 2
    return t


def _mod_kernel(c_ref, w_ref, b_ref, o_ref):
    c = c_ref[...]
    c_act = (c * jax.nn.sigmoid(c)).astype(BF16)
    o_ref[...] = jnp.dot(c_act, w_ref[...].astype(BF16),
                         preferred_element_type=F32) + b_ref[...]


def _modulation(c_pad, w, b):
    n_layers, d, n = w.shape
    rows = c_pad.shape[0]
    tn = _tile(n, 1024)
    return pl.pallas_call(
        _mod_kernel,
        grid=(n_layers, n // tn),
        in_specs=[
            pl.BlockSpec((rows, d), lambda l, j: (0, 0)),
            pl.BlockSpec((None, d, tn), lambda l, j: (l, 0, j)),
            pl.BlockSpec((None, 1, tn), lambda l, j: (l, 0, j)),
        ],
        out_specs=pl.BlockSpec((None, rows, tn), lambda l, j: (l, 0, j)),
        out_shape=jax.ShapeDtypeStruct((n_layers, rows, n), F32),
        compiler_params=_params(("arbitrary", "arbitrary")),
        name="adaln_modulation",
    )(c_pad, w, b.reshape(n_layers, 1, n))


def _norm_mod(x, g, shift, scale):
    ms = jnp.mean(x * x, axis=-1, keepdims=True)
    y = x * lax.rsqrt(ms + NORM_EPS) * g
    return y * (1.0 + scale) + shift


def _qk_norm_rope(a, gain, cosf, sinf):
    half = ROT_DIM // 2
    ms = jnp.mean(a * a, axis=-1, keepdims=True)
    y = a * lax.rsqrt(ms + NORM_EPS) * gain
    lane = lax.broadcasted_iota(jnp.int32, y.shape, 1)
    partner = jnp.where(lane < half,
                        pltpu.roll(y, HEAD_DIM - half, 1),
                        pltpu.roll(y, half, 1))
    return y * cosf + partner * sinf


def _nmm_kernel(x_ref, g_ref, sh_ref, sc_ref, w_ref, gain_ref, cos_ref, sin_ref,
                o_ref, h_scr, *, period, n_on):
    j = pl.program_id(1)

    @pl.when(j == 0)
    def _():
        h_scr[...] = _norm_mod(x_ref[...], g_ref[...], sh_ref[...], sc_ref[...]).astype(BF16)

    acc = jnp.dot(h_scr[...], w_ref[...], preferred_element_type=F32)
    qk_tile = lax.rem(j, period) < n_on

    @pl.when(qk_tile)
    def _():
        cosf = cos_ref[...]
        sinf = sin_ref[...]
        for c in range(acc.shape[1] // HEAD_DIM):
            cols = slice(c * HEAD_DIM, (c + 1) * HEAD_DIM)
            o_ref[:, cols] = _qk_norm_rope(acc[:, cols], gain_ref[:, cols], cosf, sinf).astype(o_ref.dtype)

    @pl.when(jnp.logical_not(qk_tile))
    def _():
        o_ref[...] = acc.astype(o_ref.dtype)


def _norm_mod_matmul(x, g, shift, scale, w, gains, cosf, sinf, *, seq, period, n_on):
    t, d = x.shape
    n = w.shape[1]
    tm = _tile(seq, 1024)
    tn = gains.shape[2]
    kern = functools.partial(_nmm_kernel, period=period, n_on=n_on)
    return pl.pallas_call(
        kern,
        grid=(t // tm, n // tn),
        in_specs=[
            pl.BlockSpec((tm, d), lambda i, j: (i, 0)),
            pl.BlockSpec((1, d), lambda i, j: (0, 0)),
            pl.BlockSpec((None, 1, d), lambda i, j: ((i * tm) // seq, 0, 0)),
            pl.BlockSpec((None, 1, d), lambda i, j: ((i * tm) // seq, 0, 0)),
            pl.BlockSpec((d, tn), lambda i, j: (0, j)),
            pl.BlockSpec((None, 1, tn), lambda i, j: (j, 0, 0)),
            pl.BlockSpec((tm, LANES), lambda i, j: (i, 0)),
            pl.BlockSpec((tm, LANES), lambda i, j: (i, 0)),
        ],
        out_specs=pl.BlockSpec((tm, tn), lambda i, j: (i, j)),
        out_shape=jax.ShapeDtypeStruct((t, n), BF16),
        scratch_shapes=[pltpu.VMEM((tm, d), BF16)],
        compiler_params=_params(("arbitrary", "arbitrary")),
        name="norm_mod_proj",
    )(x, g, shift, scale, w, gains, cosf, sinf)


def _dil_kernel(q_ref, kc_ref, kp_ref, vc_ref, vp_ref, o_ref, lse_ref, kbuf, vbuf, *, lq):
    chunk = pl.program_id(2)
    blk = BAND_BLOCK
    kbuf[0:blk, :] = kp_ref[...]
    kbuf[blk:, :] = kc_ref[...]
    vbuf[0:blk, :] = vp_ref[...]
    vbuf[blk:, :] = vc_ref[...]

    qi = lax.broadcasted_iota(jnp.int32, (blk, 2 * blk), 0)
    ki = lax.broadcasted_iota(jnp.int32, (blk, 2 * blk), 1)
    band = jnp.where((ki >= qi) & (ki <= qi + blk), 0.0, NEG_INF).astype(F32)
    no_prev = jnp.where(ki < blk, NEG_INF, 0.0).astype(F32)

    def body(i, carry):
        r0 = pl.multiple_of(i * blk, blk)
        first = jnp.logical_and(i == 0, chunk == 0).astype(F32)
        bias = band + first * no_prev
        for h in range(A_HEADS):
            cols = slice(h * HEAD_DIM, (h + 1) * HEAD_DIM)
            q = q_ref[pl.ds(r0, blk), cols]
            kw = kbuf[pl.ds(r0, 2 * blk), cols]
            vw = vbuf[pl.ds(r0, 2 * blk), cols]
            s = lax.dot_general(q, kw, (((1,), (1,)), ((), ())),
                                preferred_element_type=F32) + bias
            m = jnp.max(s, axis=-1, keepdims=True)
            p = jnp.exp(s - m)
            den = jnp.sum(p, axis=-1, keepdims=True)
            o = jnp.dot(p.astype(BF16), vw, preferred_element_type=F32) / den
            o_ref[pl.ds(r0, blk), cols] = o.astype(o_ref.dtype)
            lse_ref[pl.ds(r0, blk), cols] = jnp.broadcast_to(m + jnp.log(den), (blk, HEAD_DIM))
        return carry

    lax.fori_loop(0, lq // blk, body, 0)


def _dilated_attention(qkv, group, dilation, batch, seq):
    width = qkv.shape[1]
    per_row = width // A_GROUP_WIDTH
    sub_len = seq // dilation
    assert sub_len % BAND_BLOCK == 0
    lq = _tile(sub_len, 512)
    n_chunks = sub_len // lq
    blocks_per_chunk = lq // BAND_BLOCK
    qkv_r = qkv.reshape(batch, sub_len, dilation * width)

    def col(which):
        return lambda b, r, c: (b, c, r * per_row + group * 3 + which)

    def col_prev(which):
        return lambda b, r, c: (b, jnp.maximum(c * blocks_per_chunk - 1, 0),
                                r * per_row + group * 3 + which)

    cur = lambda which: pl.BlockSpec((None, lq, A_GROUP_WIDTH), col(which))
    prev = lambda which: pl.BlockSpec((None, BAND_BLOCK, A_GROUP_WIDTH), col_prev(which))
    out_spec = pl.BlockSpec((None, lq, A_GROUP_WIDTH), lambda b, r, c: (b, c, r))
    o, lse = pl.pallas_call(
        functools.partial(_dil_kernel, lq=lq),
        grid=(batch, dilation, n_chunks),
        in_specs=[cur(0), cur(1), prev(1), cur(2), prev(2)],
        out_specs=[out_spec, out_spec],
        out_shape=[jax.ShapeDtypeStruct((batch, sub_len, dilation * A_GROUP_WIDTH), BF16),
                   jax.ShapeDtypeStruct((batch, sub_len, dilation * A_GROUP_WIDTH), F32)],
        scratch_shapes=[pltpu.VMEM((lq + BAND_BLOCK, A_GROUP_WIDTH), BF16),
                        pltpu.VMEM((lq + BAND_BLOCK, A_GROUP_WIDTH), BF16)],
        compiler_params=_params(("arbitrary", "arbitrary", "arbitrary")),
        name=f"dilated_attn_d{dilation}",
    )(qkv_r, qkv_r, qkv_r, qkv_r, qkv_r)
    t = batch * seq
    return o.reshape(t, A_GROUP_WIDTH), lse.reshape(t, A_GROUP_WIDTH)


def _merge_kernel(o0, o1, o2, l0, l1, l2, out_ref):
    a0, a1, a2 = l0[...], l1[...], l2[...]
    m = jnp.maximum(jnp.maximum(a0, a1), a2)
    e0, e1, e2 = jnp.exp(a0 - m), jnp.exp(a1 - m), jnp.exp(a2 - m)
    num = e0 * o0[...].astype(F32) + e1 * o1[...].astype(F32) + e2 * o2[...].astype(F32)
    out_ref[...] = (num / (e0 + e1 + e2)).astype(out_ref.dtype)


def _merge_groups(outs, lses):
    t, w = outs[0].shape
    tm = _tile(t, 1024)
    spec = pl.BlockSpec((tm, w), lambda i: (i, 0))
    return pl.pallas_call(
        _merge_kernel,
        grid=(t // tm,),
        in_specs=[spec] * 6,
        out_specs=spec,
        out_shape=jax.ShapeDtypeStruct((t, w), BF16),
        compiler_params=_params(("arbitrary",)),
        name="merge_groups",
    )(*outs, *lses)


def _proj_res_kernel(a_ref, w_ref, x_ref, gate_ref, o_ref):
    y = jnp.dot(a_ref[...], w_ref[...], preferred_element_type=F32)
    o_ref[...] = x_ref[...] + gate_ref[...] * y


def _proj_residual(a, w, x, gate, *, seq):
    t, k = a.shape
    d = w.shape[1]
    tm = _tile(seq, 1024)
    tn = _tile(d, 1024)
    return pl.pallas_call(
        _proj_res_kernel,
        grid=(t // tm, d // tn),
        in_specs=[
            pl.BlockSpec((tm, k), lambda i, j: (i, 0)),
            pl.BlockSpec((k, tn), lambda i, j: (0, j)),
            pl.BlockSpec((tm, tn), lambda i, j: (i, j)),
            pl.BlockSpec((None, 1, tn), lambda i, j: ((i * tm) // seq, 0, j)),
        ],
        out_specs=pl.BlockSpec((tm, tn), lambda i, j: (i, j)),
        out_shape=jax.ShapeDtypeStruct((t, d), F32),
        compiler_params=_params(("arbitrary", "arbitrary")),
        name="proj_residual",
    )(a, w, x, gate)


def _mlp_kernel(x_ref, g_ref, sh_ref, sc_ref, gate_ref, w1_ref, w2_ref, o_ref, h_scr):
    f = pl.program_id(1)

    @pl.when(f == 0)
    def _():
        h_scr[...] = _norm_mod(x_ref[...], g_ref[...], sh_ref[...], sc_ref[...]).astype(BF16)

    u = jnp.dot(h_scr[...], w1_ref[...], preferred_element_type=F32)
    u = jnp.square(jnp.maximum(u, 0.0)).astype(BF16)
    y = jnp.dot(u, w2_ref[...], preferred_element_type=F32)

    @pl.when(f == 0)
    def _():
        o_ref[...] = y

    @pl.when(f != 0)
    def _():
        o_ref[...] += y

    @pl.when(f == pl.num_programs(1) - 1)
    def _():
        o_ref[...] = x_ref[...] + gate_ref[...] * o_ref[...]


def _mlp(x, g, shift, scale, gate, w1, w2, *, seq):
    t, d = x.shape
    ff = w1.shape[1]
    tm = _tile(seq, 1024)
    tf = _tile(ff, 512)
    vec = pl.BlockSpec((None, 1, d), lambda i, f: ((i * tm) // seq, 0, 0))
    return pl.pallas_call(
        _mlp_kernel,
        grid=(t // tm, ff // tf),
        in_specs=[
            pl.BlockSpec((tm, d), lambda i, f: (i, 0), pipeline_mode=pl.Buffered(1)),
            pl.BlockSpec((1, d), lambda i, f: (0, 0)),
            vec, vec, vec,
            pl.BlockSpec((d, tf), lambda i, f: (0, f)),
            pl.BlockSpec((tf, d), lambda i, f: (f, 0)),
        ],
        out_specs=pl.BlockSpec((tm, d), lambda i, f: (i, 0)),
        out_shape=jax.ShapeDtypeStruct((t, d), F32),
        scratch_shapes=[pltpu.VMEM((tm, d), BF16)],
        compiler_params=_params(("arbitrary", "arbitrary")),
        name="sq_relu_mlp",
    )(x, g, shift, scale, gate, w1, w2)


def _diff_kernel(q_ref, k_ref, v_ref, lam_ref, g_ref, o_ref, acc_scr, *, tq, lam_init):
    qi = pl.program_id(2)
    acc_scr[...] = jnp.zeros_like(acc_scr)
    row = lax.broadcasted_iota(jnp.int32, (tq, tq), 0)
    col = lax.broadcasted_iota(jnp.int32, (tq, tq), 1)
    causal = col <= row

    def step(j, carry, masked):
        k0 = pl.multiple_of(j * tq, tq)
        vblk = v_ref[pl.ds(k0, tq), :]
        out = []
        for c in range(2):
            cols = slice(c * HEAD_DIM, (c + 1) * HEAD_DIM)
            m_prev, l_prev = carry[2 * c], carry[2 * c + 1]
            s = lax.dot_general(q_ref[:, cols], k_ref[pl.ds(k0, tq), cols],
                                (((1,), (1,)), ((), ())), preferred_element_type=F32)
            if masked:
                s = jnp.where(causal, s, NEG_INF)
            m_new = jnp.maximum(m_prev, jnp.max(s, axis=-1, keepdims=True))
            alpha = jnp.exp(m_prev - m_new)
            p = jnp.exp(s - m_new)
            l_new = alpha * l_prev + jnp.sum(p, axis=-1, keepdims=True)
            acc_scr[c] = alpha * acc_scr[c] + jnp.dot(p.astype(BF16), vblk,
                                                      preferred_element_type=F32)
            out += [m_new, l_new]
        return tuple(out)

    init = (jnp.full((tq, 1), NEG_INF, F32), jnp.zeros((tq, 1), F32)) * 2
    carry = lax.fori_loop(0, qi, functools.partial(step, masked=False), init)
    _, l0, _, l1 = step(qi, carry, True)

    lp = lam_ref[...]
    lam = (jnp.exp(jnp.sum(lp[0:1] * lp[1:2], axis=-1, keepdims=True))
           - jnp.exp(jnp.sum(lp[2:3] * lp[3:4], axis=-1, keepdims=True)) + lam_init)
    o = acc_scr[0] / l0 - lam * (acc_scr[1] / l1)
    ms = jnp.mean(o * o, axis=-1, keepdims=True)
    o = o * lax.rsqrt(ms + NORM_EPS) * g_ref[...] * (1.0 - lam_init)
    o_ref[...] = o.astype(o_ref.dtype)


def _diff_attention(q, kv, lam_params, subln_g, lam_init, *, batch, seq):
    t, qw = q.shape
    heads = qw // B_V_DIM
    tq = _tile(seq, 512)
    q_r = q.reshape(batch, seq, qw)
    kv_r = kv.reshape(batch, seq, 2 * qw)
    out = pl.pallas_call(
        functools.partial(_diff_kernel, tq=tq, lam_init=lam_init),
        grid=(batch, heads, seq // tq),
        in_specs=[
            pl.BlockSpec((None, tq, B_V_DIM), lambda b, h, i: (b, i, h)),
            pl.BlockSpec((None, seq, B_V_DIM), lambda b, h, i: (b, 0, h)),
            pl.BlockSpec((None, seq, B_V_DIM), lambda b, h, i: (b, 0, heads + h)),
            pl.BlockSpec(lam_params.shape, lambda b, h, i: (0, 0)),
            pl.BlockSpec((1, B_V_DIM), lambda b, h, i: (0, 0)),
        ],
        out_specs=pl.BlockSpec((None, tq, B_V_DIM), lambda b, h, i: (b, i, h)),
        out_shape=jax.ShapeDtypeStruct((batch, seq, qw), BF16),
        scratch_shapes=[pltpu.VMEM((2, tq, B_V_DIM), F32)],
        compiler_params=_params(("arbitrary", "arbitrary", "arbitrary")),
        name="diff_attention",
    )(q_r, kv_r, kv_r, lam_params.astype(F32), subln_g.reshape(1, B_V_DIM))
    return out.reshape(t, qw)


def _rope_tables(positions):
    half = ROT_DIM // 2
    inv_freq = ROPE_THETA ** (-jnp.arange(0, ROT_DIM, 2, dtype=F32) / ROT_DIM)
    ang = positions.astype(F32).reshape(-1, 1) * inv_freq
    cos, sin = jnp.cos(ang), jnp.sin(ang)
    rest = HEAD_DIM - ROT_DIM
    cosf = jnp.concatenate([cos, cos, jnp.ones((ang.shape[0], rest), F32)], axis=-1)
    sinf = jnp.concatenate([-sin, sin, jnp.zeros((ang.shape[0], rest), F32)], axis=-1)
    return cosf, sinf


def kernel(x, c, positions, ada_w, ada_b, norm_g, a_w_qkv, a_q_gain, a_k_gain, a_w_o,
           kv_ada_w, kv_ada_b, kv_norm_g, kv_w_k, kv_w_v, kv_k_gain, b_w_q, b_q_gain,
           b_lambda, b_subln_g, b_w_o, mlp_w1, mlp_w2):
    batch, seq, d = x.shape
    depth = ada_w.shape[0]
    n_a = a_w_qkv.shape[0]
    t = batch * seq
    qk_scale = HEAD_DIM ** -0.5

    cosf, sinf = _rope_tables(positions)
    c_pad = jnp.pad(c, ((0, 8 - batch % 8 if batch % 8 else 0), (0, 0)))
    mod = _modulation(c_pad, ada_w, ada_b)[:, :batch]
    kv_mod = _modulation(c_pad, kv_ada_w[None], kv_ada_b[None])[0, :batch]

    def vecs(m, n):
        return [v.reshape(batch, 1, d) for v in jnp.split(m, n, axis=-1)]

    xf = x.reshape(t, d)
    kv = None
    for layer in range(depth):
        if layer == n_a:
            shift, scale = vecs(kv_mod, 2)
            w_kv = jnp.concatenate([kv_w_k, kv_w_v], axis=1).astype(BF16)
            n_kv = w_kv.shape[1]
            tn = _tile(n_kv // 2, 1024)
            k_gain_row = jnp.tile(kv_k_gain.reshape(-1), tn // (2 * HEAD_DIM))
            gains = jnp.broadcast_to(k_gain_row, (n_kv // tn, 1, tn))
            kv = _norm_mod_matmul(xf, kv_norm_g.reshape(1, d), shift, scale, w_kv, gains,
                                  cosf, sinf, seq=seq, period=n_kv // tn, n_on=n_kv // (2 * tn))
        shift_t, scale_t, gate_t, shift_m, scale_m, gate_m = vecs(mod[layer], N_MOD)
        g_t = norm_g[layer, 0].reshape(1, d)
        if layer < n_a:
            w_qkv = a_w_qkv[layer].astype(BF16)
            n_groups = len(A_PATTERNS)
            ones = jnp.ones((HEAD_DIM,), F32)
            gains = jnp.stack([jnp.tile(v, A_HEADS) for g in range(n_groups)
                               for v in (a_q_gain[layer, g] * qk_scale, a_k_gain[layer, g], ones)])
            qkv = _norm_mod_matmul(xf, g_t, shift_t, scale_t, w_qkv,
                                   gains.reshape(3 * n_groups, 1, A_GROUP_WIDTH),
                                   cosf, sinf, seq=seq, period=3, n_on=2)
            outs, lses = zip(*[_dilated_attention(qkv, g, dil, batch, seq)
                               for g, (_, dil) in enumerate(A_PATTERNS)])
            mixed = _merge_groups(outs, lses)
            xf = _proj_residual(mixed, a_w_o[layer].astype(BF16), xf, gate_t, seq=seq)
        else:
            j = layer - n_a
            lam_init = 0.8 - 0.6 * math.exp(-0.3 * layer)
            w_q = b_w_q[j].astype(BF16)
            n_q = w_q.shape[1]
            tn = _tile(n_q, 1024)
            q_gain_row = jnp.tile(b_q_gain[j].reshape(-1) * qk_scale, tn // (2 * HEAD_DIM))
            gains = jnp.broadcast_to(q_gain_row, (n_q // tn, 1, tn))
            q = _norm_mod_matmul(xf, g_t, shift_t, scale_t, w_q, gains, cosf, sinf,
                                 seq=seq, period=1, n_on=1)
            o = _diff_attention(q, kv, b_lambda[j], b_subln_g[j], lam_init, batch=batch, seq=seq)
            xf = _proj_residual(o, b_w_o[j].astype(BF16), xf, gate_t, seq=seq)
        xf = _mlp(xf, norm_g[layer, 1].reshape(1, d), shift_m, scale_m, gate_m,
                  mlp_w1[layer].astype(BF16), mlp_w2[layer].astype(BF16), seq=seq)
    return xf.reshape(batch, seq, d)
```

```python
import functools
import math

import jax
import jax.numpy as jnp
from jax import lax
from jax.experimental import pallas as pl
from jax.experimental.pallas import tpu as pltpu

F32 = jnp.float32
BF16 = jnp.bfloat16

HEAD_DIM = 128
ROT_DIM = HEAD_DIM // 4
ROPE_THETA = 500000.0
NORM_EPS = 1e-6
NEG_INF = -1e30
A_PATTERNS = ((128, 1), (512, 4), (2048, 16))
A_HEADS = 8
A_GROUP_WIDTH = A_HEADS * HEAD_DIM
BAND_BLOCK = 128
B_HEADS = 8
B_V_DIM = 2 * HEAD_DIM
N_MOD = 6

V7X_VMEM_LIMIT_BYTES = 56 * 1024 * 1024
LANES = 128
PROJ_ROW_SUB = 256
DIL_RANGE = 2048
DIL_UNROLL = 4


def _params(semantics):
    return pltpu.CompilerParams(dimension_semantics=semantics,
                                vmem_limit_bytes=V7X_VMEM_LIMIT_BYTES)


def _tile(n, target):
    t = min(n, target)
    while n % t:
        t //= 2
    return t


def _mod_kernel(c_ref, w_ref, b_ref, o_ref):
    c = c_ref[...]
    c_act = (c * jax.nn.sigmoid(c)).astype(BF16)
    o_ref[...] = jnp.dot(c_act, w_ref[...].astype(BF16),
                         preferred_element_type=F32) + b_ref[...]


def _modulation(c_pad, w, b):
    n_layers, d, n = w.shape
    rows = c_pad.shape[0]
    tn = _tile(n, 1024)
    return pl.pallas_call(
        _mod_kernel,
        grid=(n_layers, n // tn),
        in_specs=[
            pl.BlockSpec((rows, d), lambda l, j: (0, 0)),
            pl.BlockSpec((None, d, tn), lambda l, j: (l, 0, j)),
            pl.BlockSpec((None, 1, tn), lambda l, j: (l, 0, j)),
        ],
        out_specs=pl.BlockSpec((None, rows, tn), lambda l, j: (l, 0, j)),
        out_shape=jax.ShapeDtypeStruct((n_layers, rows, n), F32),
        compiler_params=_params(("arbitrary", "arbitrary")),
        name="adaln_modulation",
    )(c_pad, w, b.reshape(n_layers, 1, n))


def _norm_mod(x, g, shift, scale):
    ms = jnp.mean(x * x, axis=-1, keepdims=True)
    y = x * lax.rsqrt(ms + NORM_EPS) * g
    return y * (1.0 + scale) + shift


def _qk_norm_rope(a, gain, cosf, sinf):
    half = ROT_DIM // 2
    ms = jnp.mean(a * a, axis=-1, keepdims=True)
    y = a * lax.rsqrt(ms + NORM_EPS) * gain
    lane = lax.broadcasted_iota(jnp.int32, y.shape, 1)
    partner = jnp.where(lane < half,
                        pltpu.roll(y, HEAD_DIM - half, 1),
                        pltpu.roll(y, half, 1))
    return y * cosf + partner * sinf


def _nmm_kernel(x_ref, g_ref, sh_ref, sc_ref, w_ref, gain_ref, cos_ref, sin_ref,
                o_ref, h_scr, *, period, n_on):
    j = pl.program_id(1)

    @pl.when(j == 0)
    def _():
        h_scr[...] = _norm_mod(x_ref[...], g_ref[...], sh_ref[...], sc_ref[...]).astype(BF16)

    qk_tile = lax.rem(j, period) < n_on
    tm, tn = o_ref.shape
    sub = min(tm, PROJ_ROW_SUB)

    @pl.when(qk_tile)
    def _():
        for rb in range(tm // sub):
            rows = slice(rb * sub, (rb + 1) * sub)
            acc = jnp.dot(h_scr[rows, :], w_ref[...], preferred_element_type=F32)
            cosf = cos_ref[rows, :]
            sinf = sin_ref[rows, :]
            for c in range(tn // HEAD_DIM):
                cols = slice(c * HEAD_DIM, (c + 1) * HEAD_DIM)
                o_ref[rows, cols] = _qk_norm_rope(acc[:, cols], gain_ref[:, cols],
                                                  cosf, sinf).astype(o_ref.dtype)

    @pl.when(jnp.logical_not(qk_tile))
    def _():
        for rb in range(tm // sub):
            rows = slice(rb * sub, (rb + 1) * sub)
            o_ref[rows, :] = jnp.dot(h_scr[rows, :], w_ref[...],
                                     preferred_element_type=F32).astype(o_ref.dtype)


def _norm_mod_matmul(x, g, shift, scale, w, gains, cosf, sinf, *, seq, period, n_on,
                     out_dtype):
    t, d = x.shape
    n = w.shape[1]
    tm = _tile(seq, 1024)
    tn = gains.shape[2]
    kern = functools.partial(_nmm_kernel, period=period, n_on=n_on)
    return pl.pallas_call(
        kern,
        grid=(t // tm, n // tn),
        in_specs=[
            pl.BlockSpec((tm, d), lambda i, j: (i, 0)),
            pl.BlockSpec((1, d), lambda i, j: (0, 0)),
            pl.BlockSpec((None, 1, d), lambda i, j: ((i * tm) // seq, 0, 0)),
            pl.BlockSpec((None, 1, d), lambda i, j: ((i * tm) // seq, 0, 0)),
            pl.BlockSpec((d, tn), lambda i, j: (0, j)),
            pl.BlockSpec((None, 1, tn), lambda i, j: (j, 0, 0)),
            pl.BlockSpec((tm, LANES), lambda i, j: (i, 0)),
            pl.BlockSpec((tm, LANES), lambda i, j: (i, 0)),
        ],
        out_specs=pl.BlockSpec((tm, tn), lambda i, j: (i, j)),
        out_shape=jax.ShapeDtypeStruct((t, n), out_dtype),
        scratch_shapes=[pltpu.VMEM((tm, d), BF16)],
        compiler_params=_params(("arbitrary", "arbitrary")),
        name="norm_mod_proj",
    )(x, g, shift, scale, w, gains, cosf, sinf)


def _dil_attn_kernel(*refs, rng, dilations):
    n_g = len(dilations)
    ins, o_ref = refs[:5 * n_g], refs[5 * n_g]
    qd, kd, vd, og, mg, lg, oacc, macc, lacc = refs[5 * n_g + 1:]
    blk = BAND_BLOCK
    first_range = pl.program_id(1) == 0

    qi = lax.broadcasted_iota(jnp.int32, (blk, 2 * blk), 0)
    ki = lax.broadcasted_iota(jnp.int32, (blk, 2 * blk), 1)
    band = jnp.where((ki >= qi) & (ki <= qi + blk), 0.0, NEG_INF).astype(F32)
    no_prev = jnp.where(ki < blk, NEG_INF, 0.0).astype(F32)

    for g, d in enumerate(dilations):
        q_ref, kc_ref, kp_ref, vc_ref, vp_ref = ins[5 * g:5 * g + 5]
        lq = rng // d
        res_rows = blk + lq
        per_res = lq // blk

        def take(ref, r, n, d=d):
            rows = ref[...] if d == 1 else ref[pl.ds(r, n, stride=d), :]
            return rows.astype(BF16)

        for r in range(d):
            qd[r * lq:(r + 1) * lq, :] = take(q_ref, r, lq)
            kd[r * res_rows:r * res_rows + blk, :] = take(kp_ref, r, blk)
            kd[r * res_rows + blk:(r + 1) * res_rows, :] = take(kc_ref, r, lq)
            vd[r * res_rows:r * res_rows + blk, :] = take(vp_ref, r, blk)
            vd[r * res_rows + blk:(r + 1) * res_rows, :] = take(vc_ref, r, lq)

        o_dst, m_dst, l_dst = (oacc, macc, lacc) if g == 0 else (og, mg, lg)

        def unit(u, carry, d=d, per_res=per_res, res_rows=res_rows,
                 o_dst=o_dst, m_dst=m_dst, l_dst=l_dst):
            shift = per_res.bit_length() - 1
            r = lax.shift_right_logical(u, shift)
            i = lax.bitwise_and(u, per_res - 1)
            q = qd[pl.ds(pl.multiple_of(u * blk, blk), blk), :]
            k0 = pl.multiple_of(r * res_rows + i * blk, blk)
            kw = kd[pl.ds(k0, 2 * blk), :]
            vw = vd[pl.ds(k0, 2 * blk), :]
            first = jnp.logical_and(first_range, i == 0).astype(F32)
            s = lax.dot_general(q, kw, (((1,), (1,)), ((), ())),
                                preferred_element_type=F32) + (band + first * no_prev)
            m = jnp.max(s, axis=-1, keepdims=True)
            p = jnp.exp(s - m)
            den = jnp.sum(p, axis=-1, keepdims=True)
            acc = jnp.dot(p.astype(BF16), vw, preferred_element_type=F32)
            if d == 1:
                dst = pl.ds(pl.multiple_of(u * blk, blk), blk)
            else:
                dst = pl.ds(r + i * (blk * d), blk, stride=d)
            o_dst[dst, :] = acc
            m_dst[dst, :] = jnp.broadcast_to(m, (blk, HEAD_DIM))
            l_dst[dst, :] = jnp.broadcast_to(den, (blk, HEAD_DIM))
            return carry

        lax.fori_loop(0, rng // blk, unit, 0, unroll=DIL_UNROLL)

        if g > 0:
            m_old, m_grp = macc[...], mg[...]
            m_new = jnp.maximum(m_old, m_grp)
            a_old, a_grp = jnp.exp(m_old - m_new), jnp.exp(m_grp - m_new)
            o_new = a_old * oacc[...] + a_grp * og[...]
            l_new = a_old * lacc[...] + a_grp * lg[...]
            if g == n_g - 1:
                o_ref[...] = (o_new / l_new).astype(o_ref.dtype)
            else:
                oacc[...], macc[...], lacc[...] = o_new, m_new, l_new

    if n_g == 1:
        o_ref[...] = (oacc[...] / lacc[...]).astype(o_ref.dtype)


def _dilated_attention(qkv, dilations, batch, seq):
    width = qkv.shape[1]
    rng = DIL_RANGE
    blk = BAND_BLOCK
    assert seq % rng == 0 and all(rng % (d * blk) == 0 for d in dilations)
    qkv_r = qkv.reshape(batch, seq, width)
    args, specs = [], []
    for g, d in enumerate(dilations):
        prev_rows = blk * d
        per_range = rng // prev_rows

        def cur(which, g=g):
            return pl.BlockSpec((None, rng, HEAD_DIM),
                                lambda b, i, h: (b, i, (g * 3 + which) * A_HEADS + h))

        def prev(which, g=g, prev_rows=prev_rows, per_range=per_range):
            return pl.BlockSpec((None, prev_rows, HEAD_DIM),
                                lambda b, i, h: (b, jnp.maximum(i * per_range - 1, 0),
                                                 (g * 3 + which) * A_HEADS + h))

        specs += [cur(0), cur(1), prev(1), cur(2), prev(2)]
        args += [qkv_r] * 5
    kv_rows = rng + blk * max(dilations)
    out = pl.pallas_call(
        functools.partial(_dil_attn_kernel, rng=rng, dilations=tuple(dilations)),
        grid=(batch, seq // rng, A_HEADS),
        in_specs=specs,
        out_specs=pl.BlockSpec((None, rng, HEAD_DIM), lambda b, i, h: (b, i, h)),
        out_shape=jax.ShapeDtypeStruct((batch, seq, A_GROUP_WIDTH), BF16),
        scratch_shapes=[pltpu.VMEM((rng, HEAD_DIM), BF16),
                        pltpu.VMEM((kv_rows, HEAD_DIM), BF16),
                        pltpu.VMEM((kv_rows, HEAD_DIM), BF16)]
                       + [pltpu.VMEM((rng, HEAD_DIM), F32)] * 6,
        compiler_params=_params(("arbitrary", "arbitrary", "arbitrary")),
        name="dilated_attention",
    )(*args)
    return out.reshape(batch * seq, A_GROUP_WIDTH)


def _proj_res_kernel(a_ref, w_ref, x_ref, gate_ref, o_ref):
    y = jnp.dot(a_ref[...], w_ref[...], preferred_element_type=F32)
    o_ref[...] = x_ref[...] + gate_ref[...] * y


def _proj_residual(a, w, x, gate, *, seq):
    t, k = a.shape
    d = w.shape[1]
    tm = _tile(seq, 1024)
    tn = _tile(d, 1024)
    return pl.pallas_call(
        _proj_res_kernel,
        grid=(t // tm, d // tn),
        in_specs=[
            pl.BlockSpec((tm, k), lambda i, j: (i, 0)),
            pl.BlockSpec((k, tn), lambda i, j: (0, j)),
            pl.BlockSpec((tm, tn), lambda i, j: (i, j)),
            pl.BlockSpec((None, 1, tn), lambda i, j: ((i * tm) // seq, 0, j)),
        ],
        out_specs=pl.BlockSpec((tm, tn), lambda i, j: (i, j)),
        out_shape=jax.ShapeDtypeStruct((t, d), F32),
        compiler_params=_params(("arbitrary", "arbitrary")),
        name="proj_residual",
    )(a, w, x, gate)


def _mlp_kernel(x_ref, g_ref, sh_ref, sc_ref, gate_ref, w1_ref, w2_ref, o_ref, h_scr):
    f = pl.program_id(1)

    @pl.when(f == 0)
    def _():
        h_scr[...] = _norm_mod(x_ref[...], g_ref[...], sh_ref[...], sc_ref[...]).astype(BF16)
        o_ref[...] = jnp.zeros_like(o_ref)

    u = jnp.dot(h_scr[...], w1_ref[...], preferred_element_type=F32)
    u = jnp.square(jnp.maximum(u, 0.0)).astype(BF16)
    o_ref[...] += jnp.dot(u, w2_ref[...], preferred_element_type=F32)

    @pl.when(f == pl.num_programs(1) - 1)
    def _():
        o_ref[...] = x_ref[...] + gate_ref[...] * o_ref[...]


def _mlp(x, g, shift, scale, gate, w1, w2, *, seq):
    t, d = x.shape
    ff = w1.shape[1]
    tm = _tile(seq, 1024)
    tf = _tile(ff, 512)
    vec = pl.BlockSpec((None, 1, d), lambda i, f: ((i * tm) // seq, 0, 0))
    return pl.pallas_call(
        _mlp_kernel,
        grid=(t // tm, ff // tf),
        in_specs=[
            pl.BlockSpec((tm, d), lambda i, f: (i, 0), pipeline_mode=pl.Buffered(1)),
            pl.BlockSpec((1, d), lambda i, f: (0, 0)),
            vec, vec, vec,
            pl.BlockSpec((d, tf), lambda i, f: (0, f)),
            pl.BlockSpec((tf, d), lambda i, f: (f, 0)),
        ],
        out_specs=pl.BlockSpec((tm, d), lambda i, f: (i, 0)),
        out_shape=jax.ShapeDtypeStruct((t, d), F32),
        scratch_shapes=[pltpu.VMEM((tm, d), BF16)],
        compiler_params=_params(("arbitrary", "arbitrary")),
        name="sq_relu_mlp",
    )(x, g, shift, scale, gate, w1, w2)


def _diff_kernel(q_ref, k_ref, v_ref, lam_ref, g_ref, o_ref, acc_scr, *, tq, lam_init):
    qi = pl.program_id(2)
    acc_scr[...] = jnp.zeros_like(acc_scr)
    row = lax.broadcasted_iota(jnp.int32, (tq, tq), 0)
    col = lax.broadcasted_iota(jnp.int32, (tq, tq), 1)
    causal = col <= row

    def step(j, carry, masked):
        k0 = pl.multiple_of(j * tq, tq)
        vblk = v_ref[pl.ds(k0, tq), :]
        out = []
        for c in range(2):
            cols = slice(c * HEAD_DIM, (c + 1) * HEAD_DIM)
            m_prev, l_prev = carry[2 * c], carry[2 * c + 1]
            s = lax.dot_general(q_ref[:, cols], k_ref[pl.ds(k0, tq), cols],
                                (((1,), (1,)), ((), ())), preferred_element_type=F32)
            if masked:
                s = jnp.where(causal, s, NEG_INF)
            m_new = jnp.maximum(m_prev, jnp.max(s, axis=-1, keepdims=True))
            alpha = jnp.exp(m_prev - m_new)
            p = jnp.exp(s - m_new)
            l_new = alpha * l_prev + jnp.sum(p, axis=-1, keepdims=True)
            acc_scr[c] = alpha * acc_scr[c] + jnp.dot(p.astype(BF16), vblk,
                                                      preferred_element_type=F32)
            out += [m_new, l_new]
        return tuple(out)

    init = (jnp.full((tq, 1), NEG_INF, F32), jnp.zeros((tq, 1), F32)) * 2
    carry = lax.fori_loop(0, qi, functools.partial(step, masked=False), init)
    _, l0, _, l1 = step(qi, carry, True)

    lp = lam_ref[...]
    lam = (jnp.exp(jnp.sum(lp[0:1] * lp[1:2], axis=-1, keepdims=True))
           - jnp.exp(jnp.sum(lp[2:3] * lp[3:4], axis=-1, keepdims=True)) + lam_init)
    o = acc_scr[0] / l0 - lam * (acc_scr[1] / l1)
    ms = jnp.mean(o * o, axis=-1, keepdims=True)
    o = o * lax.rsqrt(ms + NORM_EPS) * g_ref[...] * (1.0 - lam_init)
    o_ref[...] = o.astype(o_ref.dtype)


def _diff_attention(q, kv, lam_params, subln_g, lam_init, *, batch, seq):
    t, qw = q.shape
    heads = qw // B_V_DIM
    tq = _tile(seq, 512)
    q_r = q.reshape(batch, seq, qw)
    kv_r = kv.reshape(batch, seq, 2 * qw)
    out = pl.pallas_call(
        functools.partial(_diff_kernel, tq=tq, lam_init=lam_init),
        grid=(batch, heads, seq // tq),
        in_specs=[
            pl.BlockSpec((None, tq, B_V_DIM), lambda b, h, i: (b, i, h)),
            pl.BlockSpec((None, seq, B_V_DIM), lambda b, h, i: (b, 0, h)),
            pl.BlockSpec((None, seq, B_V_DIM), lambda b, h, i: (b, 0, heads + h)),
            pl.BlockSpec(lam_params.shape, lambda b, h, i: (0, 0)),
            pl.BlockSpec((1, B_V_DIM), lambda b, h, i: (0, 0)),
        ],
        out_specs=pl.BlockSpec((None, tq, B_V_DIM), lambda b, h, i: (b, i, h)),
        out_shape=jax.ShapeDtypeStruct((batch, seq, qw), BF16),
        scratch_shapes=[pltpu.VMEM((2, tq, B_V_DIM), F32)],
        compiler_params=_params(("arbitrary", "arbitrary", "arbitrary")),
        name="diff_attention",
    )(q_r, kv_r, kv_r, lam_params.astype(F32), subln_g.reshape(1, B_V_DIM))
    return out.reshape(t, qw)


def _rope_tables(positions):
    inv_freq = ROPE_THETA ** (-jnp.arange(0, ROT_DIM, 2, dtype=F32) / ROT_DIM)
    ang = positions.astype(F32).reshape(-1, 1) * inv_freq
    cos, sin = jnp.cos(ang), jnp.sin(ang)
    rest = HEAD_DIM - ROT_DIM
    cosf = jnp.concatenate([cos, cos, jnp.ones((ang.shape[0], rest), F32)], axis=-1)
    sinf = jnp.concatenate([-sin, sin, jnp.zeros((ang.shape[0], rest), F32)], axis=-1)
    return cosf, sinf


def kernel(x, c, positions, ada_w, ada_b, norm_g, a_w_qkv, a_q_gain, a_k_gain, a_w_o,
           kv_ada_w, kv_ada_b, kv_norm_g, kv_w_k, kv_w_v, kv_k_gain, b_w_q, b_q_gain,
           b_lambda, b_subln_g, b_w_o, mlp_w1, mlp_w2):
    batch, seq, d = x.shape
    depth = ada_w.shape[0]
    n_a = a_w_qkv.shape[0]
    t = batch * seq
    qk_scale = HEAD_DIM ** -0.5

    cosf, sinf = _rope_tables(positions)
    c_pad = jnp.pad(c, ((0, 8 - batch % 8 if batch % 8 else 0), (0, 0)))
    mod = _modulation(c_pad, ada_w, ada_b)[:, :batch]
    kv_mod = _modulation(c_pad, kv_ada_w[None], kv_ada_b[None])[0, :batch]

    def vecs(m, n):
        return [v.reshape(batch, 1, d) for v in jnp.split(m, n, axis=-1)]

    xf = x.reshape(t, d)
    kv = None
    for layer in range(depth):
        if layer == n_a:
            shift, scale = vecs(kv_mod, 2)
            w_kv = jnp.concatenate([kv_w_k, kv_w_v], axis=1).astype(BF16)
            n_kv = w_kv.shape[1]
            tn = _tile(n_kv // 2, 1024)
            k_gain_row = jnp.tile(kv_k_gain.reshape(-1), tn // (2 * HEAD_DIM))
            gains = jnp.broadcast_to(k_gain_row, (n_kv // tn, 1, tn))
            kv = _norm_mod_matmul(xf, kv_norm_g.reshape(1, d), shift, scale, w_kv, gains,
                                  cosf, sinf, seq=seq, period=n_kv // tn, n_on=n_kv // (2 * tn),
                                  out_dtype=BF16)
        shift_t, scale_t, gate_t, shift_m, scale_m, gate_m = vecs(mod[layer], N_MOD)
        g_t = norm_g[layer, 0].reshape(1, d)
        if layer < n_a:
            w_qkv = a_w_qkv[layer].astype(BF16)
            n_groups = len(A_PATTERNS)
            ones = jnp.ones((HEAD_DIM,), F32)
            gains = jnp.stack([jnp.tile(v, A_HEADS) for g in range(n_groups)
                               for v in (a_q_gain[layer, g] * qk_scale, a_k_gain[layer, g], ones)])
            qkv = _norm_mod_matmul(xf, g_t, shift_t, scale_t, w_qkv,
                                   gains.reshape(3 * n_groups, 1, A_GROUP_WIDTH),
                                   cosf, sinf, seq=seq, period=3, n_on=2, out_dtype=F32)
            mixed = _dilated_attention(qkv, [dil for _, dil in A_PATTERNS], batch, seq)
            xf = _proj_residual(mixed, a_w_o[layer].astype(BF16), xf, gate_t, seq=seq)
        else:
            j = layer - n_a
            lam_init = 0.8 - 0.6 * math.exp(-0.3 * layer)
            w_q = b_w_q[j].astype(BF16)
            n_q = w_q.shape[1]
            tn = _tile(n_q, 1024)
            q_gain_row = jnp.tile(b_q_gain[j].reshape(-1) * qk_scale, tn // (2 * HEAD_DIM))
            gains = jnp.broadcast_to(q_gain_row, (n_q // tn, 1, tn))
            q = _norm_mod_matmul(xf, g_t, shift_t, scale_t, w_q, gains, cosf, sinf,
                                 seq=seq, period=1, n_on=1, out_dtype=BF16)
            o = _diff_attention(q, kv, b_lambda[j], b_subln_g[j], lam_init, batch=batch, seq=seq)
            xf = _proj_residual(o, b_w_o[j].astype(BF16), xf, gate_t, seq=seq)
        xf = _mlp(xf, norm_g[layer, 1].reshape(1, d), shift_m, scale_m, gate_m,
                  mlp_w1[layer].astype(BF16), mlp_w2[layer].astype(BF16), seq=seq)
    return xf.reshape(batch, seq, d)
```

```python
import functools
import math

import jax
import jax.numpy as jnp
from jax import lax
from jax.experimental import pallas as pl
from jax.experimental.pallas import tpu as pltpu

F32 = jnp.float32
BF16 = jnp.bfloat16

HEAD_DIM = 128
ROT_DIM = HEAD_DIM // 4
ROPE_THETA = 500000.0
NORM_EPS = 1e-6
NEG_INF = -1e30
A_PATTERNS = ((128, 1), (512, 4), (2048, 16))
A_HEADS = 8
A_GROUP_WIDTH = A_HEADS * HEAD_DIM
BAND_BLOCK = 128
B_HEADS = 8
B_V_DIM = 2 * HEAD_DIM
N_MOD = 6

V7X_VMEM_LIMIT_BYTES = 56 * 1024 * 1024
LANES = 128
PROJ_ROW_SUB = 256
DIL_RANGE = 2048
DIL_UNROLL = 16
LOG2E = math.log2(math.e)


def _params(semantics):
    return pltpu.CompilerParams(dimension_semantics=semantics,
                                vmem_limit_bytes=V7X_VMEM_LIMIT_BYTES)


def _tile(n, target):
    t = min(n, target)
    while n % t:
        t //= 2
    return t


def _mod_kernel(c_ref, w_ref, b_ref, o_ref):
    c = c_ref[...]
    c_act = (c * jax.nn.sigmoid(c)).astype(BF16)
    o_ref[...] = jnp.dot(c_act, w_ref[...].astype(BF16),
                         preferred_element_type=F32) + b_ref[...]


def _modulation(c_pad, w, b):
    n_layers, d, n = w.shape
    rows = c_pad.shape[0]
    tn = _tile(n, 1024)
    return pl.pallas_call(
        _mod_kernel,
        grid=(n_layers, n // tn),
        in_specs=[
            pl.BlockSpec((rows, d), lambda l, j: (0, 0)),
            pl.BlockSpec((None, d, tn), lambda l, j: (l, 0, j)),
            pl.BlockSpec((None, 1, tn), lambda l, j: (l, 0, j)),
        ],
        out_specs=pl.BlockSpec((None, rows, tn), lambda l, j: (l, 0, j)),
        out_shape=jax.ShapeDtypeStruct((n_layers, rows, n), F32),
        compiler_params=_params(("arbitrary", "arbitrary")),
        name="adaln_modulation",
    )(c_pad, w, b.reshape(n_layers, 1, n))


def _norm_mod(x, g, shift, scale):
    ms = jnp.mean(x * x, axis=-1, keepdims=True)
    y = x * lax.rsqrt(ms + NORM_EPS) * g
    return y * (1.0 + scale) + shift


def _qk_norm_rope(a, gain, cosf, sinf):
    half = ROT_DIM // 2
    ms = jnp.mean(a * a, axis=-1, keepdims=True)
    y = a * lax.rsqrt(ms + NORM_EPS) * gain
    lane = lax.broadcasted_iota(jnp.int32, y.shape, 1)
    partner = jnp.where(lane < half,
                        pltpu.roll(y, HEAD_DIM - half, 1),
                        pltpu.roll(y, half, 1))
    return y * cosf + partner * sinf


def _nmm_kernel(x_ref, g_ref, sh_ref, sc_ref, w_ref, gain_ref, cos_ref, sin_ref,
                o_ref, h_scr, *, period, n_on):
    j = pl.program_id(1)

    @pl.when(j == 0)
    def _():
        h_scr[...] = _norm_mod(x_ref[...], g_ref[...], sh_ref[...], sc_ref[...]).astype(BF16)

    qk_tile = lax.rem(j, period) < n_on
    tm, tn = o_ref.shape
    sub = min(tm, PROJ_ROW_SUB)

    @pl.when(qk_tile)
    def _():
        for rb in range(tm // sub):
            rows = slice(rb * sub, (rb + 1) * sub)
            acc = jnp.dot(h_scr[rows, :], w_ref[...], preferred_element_type=F32)
            cosf = cos_ref[rows, :]
            sinf = sin_ref[rows, :]
            for c in range(tn // HEAD_DIM):
                cols = slice(c * HEAD_DIM, (c + 1) * HEAD_DIM)
                o_ref[rows, cols] = _qk_norm_rope(acc[:, cols], gain_ref[:, cols],
                                                  cosf, sinf).astype(o_ref.dtype)

    @pl.when(jnp.logical_not(qk_tile))
    def _():
        for rb in range(tm // sub):
            rows = slice(rb * sub, (rb + 1) * sub)
            o_ref[rows, :] = jnp.dot(h_scr[rows, :], w_ref[...],
                                     preferred_element_type=F32).astype(o_ref.dtype)


def _norm_mod_matmul(x, g, shift, scale, w, layer, gains, cosf, sinf, *, seq, period, n_on,
                     out_dtype):
    t, d = x.shape
    n = w.shape[2]
    tm = _tile(seq, 1024)
    tn = gains.shape[2]
    kern = functools.partial(_nmm_kernel, period=period, n_on=n_on)
    return pl.pallas_call(
        kern,
        grid=(t // tm, n // tn),
        in_specs=[
            pl.BlockSpec((tm, d), lambda i, j: (i, 0)),
            pl.BlockSpec((1, d), lambda i, j: (0, 0)),
            pl.BlockSpec((None, 1, d), lambda i, j: ((i * tm) // seq, 0, 0)),
            pl.BlockSpec((None, 1, d), lambda i, j: ((i * tm) // seq, 0, 0)),
            pl.BlockSpec((None, d, tn), lambda i, j: (layer, 0, j)),
            pl.BlockSpec((None, 1, tn), lambda i, j: (j, 0, 0)),
            pl.BlockSpec((tm, LANES), lambda i, j: (i, 0)),
            pl.BlockSpec((tm, LANES), lambda i, j: (i, 0)),
        ],
        out_specs=pl.BlockSpec((tm, tn), lambda i, j: (i, j)),
        out_shape=jax.ShapeDtypeStruct((t, n), out_dtype),
        scratch_shapes=[pltpu.VMEM((tm, d), BF16)],
        compiler_params=_params(("arbitrary", "arbitrary")),
        name="norm_mod_proj",
    )(x, g, shift, scale, w, gains, cosf, sinf)


def _dil_attn_kernel(*refs, rng, dilations):
    n_g = len(dilations)
    ins, o_ref = refs[:3 * n_g], refs[3 * n_g]
    scratch = refs[3 * n_g + 1:]
    qd, og, mg, lg, oacc, macc, lacc = scratch[:7]
    kv_scr = scratch[7:]
    blk = BAND_BLOCK
    first_range = pl.program_id(2) == 0

    qi = lax.broadcasted_iota(jnp.int32, (blk, 2 * blk), 0)
    ki = lax.broadcasted_iota(jnp.int32, (blk, 2 * blk), 1)
    band = jnp.where((ki >= qi) & (ki <= qi + blk), 0.0, NEG_INF).astype(F32)
    no_prev = jnp.where(ki < blk, NEG_INF, 0.0).astype(F32)

    for g, d in enumerate(dilations):
        q_ref, k_ref, v_ref = ins[3 * g:3 * g + 3]
        kd, vd = kv_scr[2 * g:2 * g + 2]
        lq = rng // d
        res_rows = blk + lq
        per_res = lq // blk

        @pl.when(first_range)
        def _(kd=kd, vd=vd, d=d, res_rows=res_rows):
            for r in range(d):
                kd[r * res_rows:r * res_rows + blk, :] = jnp.zeros((blk, HEAD_DIM), BF16)
                vd[r * res_rows:r * res_rows + blk, :] = jnp.zeros((blk, HEAD_DIM), BF16)

        def take(ref, r, n, d=d):
            rows = ref[...] if d == 1 else ref[pl.ds(r, n, stride=d), :]
            return rows.astype(BF16)

        for r in range(d):
            qd[r * lq:(r + 1) * lq, :] = take(q_ref, r, lq)
            kd[r * res_rows + blk:(r + 1) * res_rows, :] = take(k_ref, r, lq)
            vd[r * res_rows + blk:(r + 1) * res_rows, :] = take(v_ref, r, lq)

        o_dst, m_dst, l_dst = (oacc, macc, lacc) if g == 0 else (og, mg, lg)

        def unit(u, carry, d=d, per_res=per_res, res_rows=res_rows, kd=kd, vd=vd,
                 o_dst=o_dst, m_dst=m_dst, l_dst=l_dst):
            shift = per_res.bit_length() - 1
            r = lax.shift_right_logical(u, shift)
            i = lax.bitwise_and(u, per_res - 1)
            q = qd[pl.ds(pl.multiple_of(u * blk, blk), blk), :]
            k0 = pl.multiple_of(r * res_rows + i * blk, blk)
            kw = kd[pl.ds(k0, 2 * blk), :]
            vw = vd[pl.ds(k0, 2 * blk), :]
            first = jnp.logical_and(first_range, i == 0).astype(F32)
            s = lax.dot_general(q, kw, (((1,), (1,)), ((), ())),
                                preferred_element_type=F32) + (band + first * no_prev)
            m = jnp.max(s, axis=-1, keepdims=True)
            p = jnp.exp2(s - m)
            den = jnp.sum(p, axis=-1, keepdims=True)
            acc = jnp.dot(p.astype(BF16), vw, preferred_element_type=F32)
            if d == 1:
                dst = pl.ds(pl.multiple_of(u * blk, blk), blk)
            else:
                dst = pl.ds(r + i * (blk * d), blk, stride=d)
            o_dst[dst, :] = acc
            m_dst[dst, :] = jnp.broadcast_to(m, (blk, HEAD_DIM))
            l_dst[dst, :] = jnp.broadcast_to(den, (blk, HEAD_DIM))
            return carry

        lax.fori_loop(0, rng // blk, unit, 0, unroll=DIL_UNROLL)

        for r in range(d):
            kd[r * res_rows:r * res_rows + blk, :] = kd[(r + 1) * res_rows - blk:(r + 1) * res_rows, :]
            vd[r * res_rows:r * res_rows + blk, :] = vd[(r + 1) * res_rows - blk:(r + 1) * res_rows, :]

        if g > 0:
            m_old, m_grp = macc[...], mg[...]
            m_new = jnp.maximum(m_old, m_grp)
            a_old, a_grp = jnp.exp2(m_old - m_new), jnp.exp2(m_grp - m_new)
            o_new = a_old * oacc[...] + a_grp * og[...]
            l_new = a_old * lacc[...] + a_grp * lg[...]
            if g == n_g - 1:
                o_ref[...] = (o_new / l_new).astype(o_ref.dtype)
            else:
                oacc[...], macc[...], lacc[...] = o_new, m_new, l_new

    if n_g == 1:
        o_ref[...] = (oacc[...] / lacc[...]).astype(o_ref.dtype)


def _dilated_attention(qkv, dilations, batch, seq):
    width = qkv.shape[1]
    rng = DIL_RANGE
    blk = BAND_BLOCK
    assert seq % rng == 0 and all(rng % (d * blk) == 0 for d in dilations)
    qkv_r = qkv.reshape(batch, seq, width)

    def spec(g, which):
        return pl.BlockSpec((None, rng, HEAD_DIM),
                            lambda b, h, i: (b, i, (g * 3 + which) * A_HEADS + h))

    specs = [spec(g, which) for g in range(len(dilations)) for which in range(3)]
    kv_scratch = [pltpu.VMEM((rng + blk * d, HEAD_DIM), BF16) for d in dilations for _ in range(2)]
    out = pl.pallas_call(
        functools.partial(_dil_attn_kernel, rng=rng, dilations=tuple(dilations)),
        grid=(batch, A_HEADS, seq // rng),
        in_specs=specs,
        out_specs=pl.BlockSpec((None, rng, HEAD_DIM), lambda b, h, i: (b, i, h)),
        out_shape=jax.ShapeDtypeStruct((batch, seq, A_GROUP_WIDTH), BF16),
        scratch_shapes=[pltpu.VMEM((rng, HEAD_DIM), BF16)]
                       + [pltpu.VMEM((rng, HEAD_DIM), F32)] * 6 + kv_scratch,
        compiler_params=_params(("arbitrary", "arbitrary", "arbitrary")),
        name="dilated_attention",
    )(*([qkv_r] * len(specs)))
    return out.reshape(batch * seq, A_GROUP_WIDTH)


def _proj_res_kernel(a_ref, w_ref, x_ref, gate_ref, o_ref):
    y = jnp.dot(a_ref[...], w_ref[...], preferred_element_type=F32)
    o_ref[...] = x_ref[...] + gate_ref[...] * y


def _proj_residual(a, w, layer, x, gate, *, seq):
    t, k = a.shape
    d = w.shape[2]
    tm = _tile(seq, 1024)
    tn = _tile(d, 1024)
    return pl.pallas_call(
        _proj_res_kernel,
        grid=(t // tm, d // tn),
        in_specs=[
            pl.BlockSpec((tm, k), lambda i, j: (i, 0)),
            pl.BlockSpec((None, k, tn), lambda i, j: (layer, 0, j)),
            pl.BlockSpec((tm, tn), lambda i, j: (i, j)),
            pl.BlockSpec((None, 1, tn), lambda i, j: ((i * tm) // seq, 0, j)),
        ],
        out_specs=pl.BlockSpec((tm, tn), lambda i, j: (i, j)),
        out_shape=jax.ShapeDtypeStruct((t, d), F32),
        compiler_params=_params(("arbitrary", "arbitrary")),
        name="proj_residual",
    )(a, w, x, gate)


def _mlp_kernel(x_ref, g_ref, sh_ref, sc_ref, gate_ref, w1_ref, w2_ref, o_ref, h_scr):
    f = pl.program_id(1)

    @pl.when(f == 0)
    def _():
        h_scr[...] = _norm_mod(x_ref[...], g_ref[...], sh_ref[...], sc_ref[...]).astype(BF16)
        o_ref[...] = jnp.zeros_like(o_ref)

    u = jnp.dot(h_scr[...], w1_ref[...], preferred_element_type=F32)
    u = jnp.square(jnp.maximum(u, 0.0)).astype(BF16)
    o_ref[...] += jnp.dot(u, w2_ref[...], preferred_element_type=F32)

    @pl.when(f == pl.num_programs(1) - 1)
    def _():
        o_ref[...] = x_ref[...] + gate_ref[...] * o_ref[...]


def _mlp(x, g, shift, scale, gate, w1, w2, layer, *, seq):
    t, d = x.shape
    ff = w1.shape[2]
    tm = _tile(seq, 1024)
    tf = _tile(ff, 512)
    vec = pl.BlockSpec((None, 1, d), lambda i, f: ((i * tm) // seq, 0, 0))
    return pl.pallas_call(
        _mlp_kernel,
        grid=(t // tm, ff // tf),
        in_specs=[
            pl.BlockSpec((tm, d), lambda i, f: (i, 0), pipeline_mode=pl.Buffered(1)),
            pl.BlockSpec((1, d), lambda i, f: (0, 0)),
            vec, vec, vec,
            pl.BlockSpec((None, d, tf), lambda i, f: (layer, 0, f)),
            pl.BlockSpec((None, tf, d), lambda i, f: (layer, f, 0)),
        ],
        out_specs=pl.BlockSpec((tm, d), lambda i, f: (i, 0)),
        out_shape=jax.ShapeDtypeStruct((t, d), F32),
        scratch_shapes=[pltpu.VMEM((tm, d), BF16)],
        compiler_params=_params(("arbitrary", "arbitrary")),
        name="sq_relu_mlp",
    )(x, g, shift, scale, gate, w1, w2)


def _diff_kernel(q_ref, k_ref, v_ref, lam_ref, g_ref, o_ref, acc_scr, s_a, s_b, *, tq, lam_init):
    qi = pl.program_id(2)
    acc_scr[...] = jnp.zeros_like(acc_scr)
    row = lax.broadcasted_iota(jnp.int32, (tq, tq), 0)
    col = lax.broadcasted_iota(jnp.int32, (tq, tq), 1)
    causal = col <= row

    def scores(j, s_ref):
        k0 = pl.multiple_of(j * tq, tq)
        for c in range(2):
            cols = slice(c * HEAD_DIM, (c + 1) * HEAD_DIM)
            s_ref[c] = lax.dot_general(q_ref[:, cols], k_ref[pl.ds(k0, tq), cols],
                                       (((1,), (1,)), ((), ())), preferred_element_type=F32)

    def update(j, s_ref, stats, masked):
        k0 = pl.multiple_of(j * tq, tq)
        vblk = v_ref[pl.ds(k0, tq), :]
        out = []
        for c in range(2):
            m_prev, l_prev = stats[2 * c], stats[2 * c + 1]
            s = jnp.where(causal, s_ref[c], NEG_INF) if masked else s_ref[c]
            m_new = jnp.maximum(m_prev, jnp.max(s, axis=-1, keepdims=True))
            alpha = jnp.exp2(m_prev - m_new)
            p = jnp.exp2(s - m_new)
            l_new = alpha * l_prev + jnp.sum(p, axis=-1, keepdims=True)
            acc_scr[c] = alpha * acc_scr[c] + jnp.dot(p.astype(BF16), vblk,
                                                      preferred_element_type=F32)
            out += [m_new, l_new]
        return tuple(out)

    def pair(t, stats):
        j = 2 * t
        scores(j + 1, s_b)
        stats = update(j, s_a, stats, False)
        scores(j + 2, s_a)
        return update(j + 1, s_b, stats, False)

    def odd_tail(stats):
        stats = update(qi - 1, s_a, stats, False)
        scores(qi, s_a)
        return stats

    stats0 = (jnp.full((tq, 1), NEG_INF, F32), jnp.zeros((tq, 1), F32)) * 2
    scores(0, s_a)
    stats = lax.fori_loop(0, qi // 2, pair, stats0)
    stats = lax.cond(qi % 2 == 1, odd_tail, lambda st: st, stats)
    _, l0, _, l1 = update(qi, s_a, stats, True)

    lp = lam_ref[...]
    lam = (jnp.exp(jnp.sum(lp[0:1] * lp[1:2], axis=-1, keepdims=True))
           - jnp.exp(jnp.sum(lp[2:3] * lp[3:4], axis=-1, keepdims=True)) + lam_init)
    o = acc_scr[0] / l0 - lam * (acc_scr[1] / l1)
    ms = jnp.mean(o * o, axis=-1, keepdims=True)
    o = o * lax.rsqrt(ms + NORM_EPS) * g_ref[...] * (1.0 - lam_init)
    o_ref[...] = o.astype(o_ref.dtype)


def _diff_attention(q, kv, lam_params, subln_g, lam_init, *, batch, seq):
    t, qw = q.shape
    heads = qw // B_V_DIM
    tq = _tile(seq, 512)
    q_r = q.reshape(batch, seq, qw)
    kv_r = kv.reshape(batch, seq, 2 * qw)
    out = pl.pallas_call(
        functools.partial(_diff_kernel, tq=tq, lam_init=lam_init),
        grid=(batch, heads, seq // tq),
        in_specs=[
            pl.BlockSpec((None, tq, B_V_DIM), lambda b, h, i: (b, i, h)),
            pl.BlockSpec((None, seq, B_V_DIM), lambda b, h, i: (b, 0, h)),
            pl.BlockSpec((None, seq, B_V_DIM), lambda b, h, i: (b, 0, heads + h)),
            pl.BlockSpec(lam_params.shape, lambda b, h, i: (0, 0)),
            pl.BlockSpec((1, B_V_DIM), lambda b, h, i: (0, 0)),
        ],
        out_specs=pl.BlockSpec((None, tq, B_V_DIM), lambda b, h, i: (b, i, h)),
        out_shape=jax.ShapeDtypeStruct((batch, seq, qw), BF16),
        scratch_shapes=[pltpu.VMEM((2, tq, B_V_DIM), F32),
                        pltpu.VMEM((2, tq, tq), F32), pltpu.VMEM((2, tq, tq), F32)],
        compiler_params=_params(("arbitrary", "arbitrary", "arbitrary")),
        name="diff_attention",
    )(q_r, kv_r, kv_r, lam_params.astype(F32), subln_g.reshape(1, B_V_DIM))
    return out.reshape(t, qw)


def _rope_tables(positions):
    inv_freq = ROPE_THETA ** (-jnp.arange(0, ROT_DIM, 2, dtype=F32) / ROT_DIM)
    ang = positions.astype(F32).reshape(-1, 1) * inv_freq
    cos, sin = jnp.cos(ang), jnp.sin(ang)
    rest = HEAD_DIM - ROT_DIM
    cosf = jnp.concatenate([cos, cos, jnp.ones((ang.shape[0], rest), F32)], axis=-1)
    sinf = jnp.concatenate([-sin, sin, jnp.zeros((ang.shape[0], rest), F32)], axis=-1)
    return cosf, sinf


def kernel(x, c, positions, ada_w, ada_b, norm_g, a_w_qkv, a_q_gain, a_k_gain, a_w_o,
           kv_ada_w, kv_ada_b, kv_norm_g, kv_w_k, kv_w_v, kv_k_gain, b_w_q, b_q_gain,
           b_lambda, b_subln_g, b_w_o, mlp_w1, mlp_w2):
    batch, seq, d = x.shape
    depth = ada_w.shape[0]
    n_a = a_w_qkv.shape[0]
    t = batch * seq
    qk_scale = HEAD_DIM ** -0.5 * LOG2E

    cosf, sinf = _rope_tables(positions)
    c_pad = jnp.pad(c, ((0, 8 - batch % 8 if batch % 8 else 0), (0, 0)))
    mod = _modulation(c_pad, ada_w, ada_b)[:, :batch]
    kv_mod = _modulation(c_pad, kv_ada_w[None], kv_ada_b[None])[0, :batch]

    def vecs(m, n):
        return [v.reshape(batch, 1, d) for v in jnp.split(m, n, axis=-1)]

    w_qkv, w_ao = a_w_qkv.astype(BF16), a_w_o.astype(BF16)
    w_bq, w_bo = b_w_q.astype(BF16), b_w_o.astype(BF16)
    w1, w2 = mlp_w1.astype(BF16), mlp_w2.astype(BF16)
    xf = x.reshape(t, d)
    kv = None
    for layer in range(depth):
        if layer == n_a:
            shift, scale = vecs(kv_mod, 2)
            w_kv = jnp.concatenate([kv_w_k, kv_w_v], axis=1).astype(BF16)[None]
            n_kv = w_kv.shape[2]
            tn = _tile(n_kv // 2, 1024)
            k_gain_row = jnp.tile(kv_k_gain.reshape(-1), tn // (2 * HEAD_DIM))
            gains = jnp.broadcast_to(k_gain_row, (n_kv // tn, 1, tn))
            kv = _norm_mod_matmul(xf, kv_norm_g.reshape(1, d), shift, scale, w_kv, 0, gains,
                                  cosf, sinf, seq=seq, period=n_kv // tn, n_on=n_kv // (2 * tn),
                                  out_dtype=BF16)
        shift_t, scale_t, gate_t, shift_m, scale_m, gate_m = vecs(mod[layer], N_MOD)
        g_t = norm_g[layer, 0].reshape(1, d)
        if layer < n_a:
            n_groups = len(A_PATTERNS)
            ones = jnp.ones((HEAD_DIM,), F32)
            gains = jnp.stack([jnp.tile(v, A_HEADS) for g in range(n_groups)
                               for v in (a_q_gain[layer, g] * qk_scale, a_k_gain[layer, g], ones)])
            qkv = _norm_mod_matmul(xf, g_t, shift_t, scale_t, w_qkv, layer,
                                   gains.reshape(3 * n_groups, 1, A_GROUP_WIDTH),
                                   cosf, sinf, seq=seq, period=3, n_on=2, out_dtype=F32)
            mixed = _dilated_attention(qkv, [dil for _, dil in A_PATTERNS], batch, seq)
            xf = _proj_residual(mixed, w_ao, layer, xf, gate_t, seq=seq)
        else:
            j = layer - n_a
            lam_init = 0.8 - 0.6 * math.exp(-0.3 * layer)
            n_q = w_bq.shape[2]
            tn = _tile(n_q, 1024)
            q_gain_row = jnp.tile(b_q_gain[j].reshape(-1) * qk_scale, tn // (2 * HEAD_DIM))
            gains = jnp.broadcast_to(q_gain_row, (n_q // tn, 1, tn))
            q = _norm_mod_matmul(xf, g_t, shift_t, scale_t, w_bq, j, gains, cosf, sinf,
                                 seq=seq, period=1, n_on=1, out_dtype=BF16)
            o = _diff_attention(q, kv, b_lambda[j], b_subln_g[j], lam_init, batch=batch, seq=seq)
            xf = _proj_residual(o, w_bo, j, xf, gate_t, seq=seq)
        xf = _mlp(xf, norm_g[layer, 1].reshape(1, d), shift_m, scale_m, gate_m,
                  w1, w2, layer, seq=seq)
    return xf.reshape(batch, seq, d)
```

```python
import functools
import math

import jax
import jax.numpy as jnp
from jax import lax
from jax.experimental import pallas as pl
from jax.experimental.pallas import tpu as pltpu

F32 = jnp.float32
BF16 = jnp.bfloat16

HEAD_DIM = 128
ROT_DIM = HEAD_DIM // 4
ROPE_THETA = 500000.0
NORM_EPS = 1e-6
NEG_INF = -1e30
A_PATTERNS = ((128, 1), (512, 4), (2048, 16))
A_HEADS = 8
A_GROUP_WIDTH = A_HEADS * HEAD_DIM
BAND_BLOCK = 128
B_HEADS = 8
B_V_DIM = 2 * HEAD_DIM
N_MOD = 6

V7X_VMEM_LIMIT_BYTES = 56 * 1024 * 1024
LANES = 128
PROJ_ROW_SUB = 256
NORM_ROW_SUB = 128
DIL_RANGE = 2048
DIL_UNROLL = 16
LOG2E = math.log2(math.e)


def _params(semantics):
    return pltpu.CompilerParams(dimension_semantics=semantics,
                                vmem_limit_bytes=V7X_VMEM_LIMIT_BYTES)


def _tile(n, target):
    t = min(n, target)
    while n % t:
        t //= 2
    return t


def _mod_kernel(c_ref, w_ref, b_ref, o_ref):
    c = c_ref[...]
    c_act = (c * jax.nn.sigmoid(c)).astype(BF16)
    o_ref[...] = jnp.dot(c_act, w_ref[...].astype(BF16),
                         preferred_element_type=F32) + b_ref[...]


def _modulation(c_pad, w, b):
    n_layers, d, n = w.shape
    rows = c_pad.shape[0]
    tn = _tile(n, 1024)
    return pl.pallas_call(
        _mod_kernel,
        grid=(n_layers, n // tn),
        in_specs=[
            pl.BlockSpec((rows, d), lambda l, j: (0, 0)),
            pl.BlockSpec((None, d, tn), lambda l, j: (l, 0, j)),
            pl.BlockSpec((None, 1, tn), lambda l, j: (l, 0, j)),
        ],
        out_specs=pl.BlockSpec((None, rows, tn), lambda l, j: (l, 0, j)),
        out_shape=jax.ShapeDtypeStruct((n_layers, rows, n), F32),
        compiler_params=_params(("arbitrary", "arbitrary")),
        name="adaln_modulation",
    )(c_pad, w, b.reshape(n_layers, 1, n))


def _norm_mod(x, g, shift, scale):
    ms = jnp.mean(x * x, axis=-1, keepdims=True)
    y = x * lax.rsqrt(ms + NORM_EPS) * g
    return y * (1.0 + scale) + shift


def _norm_mod_to(h_scr, x_ref, g_ref, sh_ref, sc_ref):
    g, shift, scale = g_ref[...], sh_ref[...], sc_ref[...]
    tm = x_ref.shape[0]
    sub = min(tm, NORM_ROW_SUB)
    for rb in range(tm // sub):
        rows = slice(rb * sub, (rb + 1) * sub)
        h_scr[rows, :] = _norm_mod(x_ref[rows, :], g, shift, scale).astype(BF16)


def _qk_norm_rope(a, gain, cosf, sinf):
    half = ROT_DIM // 2
    ms = jnp.mean(a * a, axis=-1, keepdims=True)
    y = a * lax.rsqrt(ms + NORM_EPS) * gain
    lane = lax.broadcasted_iota(jnp.int32, y.shape, 1)
    partner = jnp.where(lane < half,
                        pltpu.roll(y, HEAD_DIM - half, 1),
                        pltpu.roll(y, half, 1))
    return y * cosf + partner * sinf


def _nmm_kernel(x_ref, g_ref, sh_ref, sc_ref, w_ref, gain_ref, cos_ref, sin_ref,
                o_ref, h_scr, *, period, n_on):
    j = pl.program_id(1)

    @pl.when(j == 0)
    def _():
        _norm_mod_to(h_scr, x_ref, g_ref, sh_ref, sc_ref)

    qk_tile = lax.rem(j, period) < n_on
    tm, tn = o_ref.shape
    sub = min(tm, PROJ_ROW_SUB)

    @pl.when(qk_tile)
    def _():
        for rb in range(tm // sub):
            rows = slice(rb * sub, (rb + 1) * sub)
            acc = jnp.dot(h_scr[rows, :], w_ref[...], preferred_element_type=F32)
            cosf = cos_ref[rows, :]
            sinf = sin_ref[rows, :]
            for c in range(tn // HEAD_DIM):
                cols = slice(c * HEAD_DIM, (c + 1) * HEAD_DIM)
                o_ref[rows, cols] = _qk_norm_rope(acc[:, cols], gain_ref[:, cols],
                                                  cosf, sinf).astype(o_ref.dtype)

    @pl.when(jnp.logical_not(qk_tile))
    def _():
        for rb in range(tm // sub):
            rows = slice(rb * sub, (rb + 1) * sub)
            o_ref[rows, :] = jnp.dot(h_scr[rows, :], w_ref[...],
                                     preferred_element_type=F32).astype(o_ref.dtype)


def _norm_mod_matmul(x, g, shift, scale, w, layer, gains, cosf, sinf, *, seq, period, n_on,
                     out_dtype):
    t, d = x.shape
    n = w.shape[2]
    tm = _tile(seq, 1024)
    tn = gains.shape[2]
    kern = functools.partial(_nmm_kernel, period=period, n_on=n_on)
    return pl.pallas_call(
        kern,
        grid=(t // tm, n // tn),
        in_specs=[
            pl.BlockSpec((tm, d), lambda i, j: (i, 0)),
            pl.BlockSpec((1, d), lambda i, j: (0, 0)),
            pl.BlockSpec((None, 1, d), lambda i, j: ((i * tm) // seq, 0, 0)),
            pl.BlockSpec((None, 1, d), lambda i, j: ((i * tm) // seq, 0, 0)),
            pl.BlockSpec((None, d, tn), lambda i, j: (layer, 0, j)),
            pl.BlockSpec((None, 1, tn), lambda i, j: (j, 0, 0)),
            pl.BlockSpec((tm, LANES), lambda i, j: (i, 0)),
            pl.BlockSpec((tm, LANES), lambda i, j: (i, 0)),
        ],
        out_specs=pl.BlockSpec((tm, tn), lambda i, j: (i, j)),
        out_shape=jax.ShapeDtypeStruct((t, n), out_dtype),
        scratch_shapes=[pltpu.VMEM((tm, d), BF16)],
        compiler_params=_params(("arbitrary", "arbitrary")),
        name="norm_mod_proj",
    )(x, g, shift, scale, w, gains, cosf, sinf)


def _dil_attn_kernel(*refs, rng, dilations):
    n_g = len(dilations)
    ins, o_ref = refs[:3 * n_g], refs[3 * n_g]
    scratch = refs[3 * n_g + 1:]
    qd, og, mg, lg, oacc, macc, lacc = scratch[:7]
    kv_scr = scratch[7:]
    blk = BAND_BLOCK
    first_range = pl.program_id(2) == 0

    qi = lax.broadcasted_iota(jnp.int32, (blk, 2 * blk), 0)
    ki = lax.broadcasted_iota(jnp.int32, (blk, 2 * blk), 1)
    band = jnp.where((ki >= qi) & (ki <= qi + blk), 0.0, NEG_INF).astype(F32)
    no_prev = jnp.where(ki < blk, NEG_INF, 0.0).astype(F32)

    for g, d in enumerate(dilations):
        q_ref, k_ref, v_ref = ins[3 * g:3 * g + 3]
        kd, vd = kv_scr[2 * g:2 * g + 2]
        lq = rng // d
        res_rows = blk + lq
        per_res = lq // blk

        @pl.when(first_range)
        def _(kd=kd, vd=vd, d=d, res_rows=res_rows):
            for r in range(d):
                kd[r * res_rows:r * res_rows + blk, :] = jnp.zeros((blk, HEAD_DIM), BF16)
                vd[r * res_rows:r * res_rows + blk, :] = jnp.zeros((blk, HEAD_DIM), BF16)

        def take(ref, r, n, d=d):
            rows = ref[...] if d == 1 else ref[pl.ds(r, n, stride=d), :]
            return rows.astype(BF16)

        for r in range(d):
            qd[r * lq:(r + 1) * lq, :] = take(q_ref, r, lq)
            kd[r * res_rows + blk:(r + 1) * res_rows, :] = take(k_ref, r, lq)
            vd[r * res_rows + blk:(r + 1) * res_rows, :] = take(v_ref, r, lq)

        o_dst, m_dst, l_dst = (oacc, macc, lacc) if g == 0 else (og, mg, lg)

        def unit(u, carry, d=d, per_res=per_res, res_rows=res_rows, kd=kd, vd=vd,
                 o_dst=o_dst, m_dst=m_dst, l_dst=l_dst):
            shift = per_res.bit_length() - 1
            r = lax.shift_right_logical(u, shift)
            i = lax.bitwise_and(u, per_res - 1)
            q = qd[pl.ds(pl.multiple_of(u * blk, blk), blk), :]
            k0 = pl.multiple_of(r * res_rows + i * blk, blk)
            kw = kd[pl.ds(k0, 2 * blk), :]
            vw = vd[pl.ds(k0, 2 * blk), :]
            first = jnp.logical_and(first_range, i == 0).astype(F32)
            s = lax.dot_general(q, kw, (((1,), (1,)), ((), ())),
                                preferred_element_type=F32) + (band + first * no_prev)
            m = jnp.max(s, axis=-1, keepdims=True)
            p = jnp.exp2(s - m)
            den = jnp.sum(p, axis=-1, keepdims=True)
            acc = jnp.dot(p.astype(BF16), vw, preferred_element_type=F32)
            if d == 1:
                dst = pl.ds(pl.multiple_of(u * blk, blk), blk)
            else:
                dst = pl.ds(r + i * (blk * d), blk, stride=d)
            o_dst[dst, :] = acc
            m_dst[dst, :] = jnp.broadcast_to(m, (blk, HEAD_DIM))
            l_dst[dst, :] = jnp.broadcast_to(den, (blk, HEAD_DIM))
            return carry

        lax.fori_loop(0, rng // blk, unit, 0, unroll=DIL_UNROLL)

        for r in range(d):
            kd[r * res_rows:r * res_rows + blk, :] = kd[(r + 1) * res_rows - blk:(r + 1) * res_rows, :]
            vd[r * res_rows:r * res_rows + blk, :] = vd[(r + 1) * res_rows - blk:(r + 1) * res_rows, :]

        if g > 0:
            m_old, m_grp = macc[...], mg[...]
            m_new = jnp.maximum(m_old, m_grp)
            a_old, a_grp = jnp.exp2(m_old - m_new), jnp.exp2(m_grp - m_new)
            o_new = a_old * oacc[...] + a_grp * og[...]
            l_new = a_old * lacc[...] + a_grp * lg[...]
            if g == n_g - 1:
                o_ref[...] = (o_new / l_new).astype(o_ref.dtype)
            else:
                oacc[...], macc[...], lacc[...] = o_new, m_new, l_new

    if n_g == 1:
        o_ref[...] = (oacc[...] / lacc[...]).astype(o_ref.dtype)


def _dilated_attention(qkv, dilations, batch, seq):
    width = qkv.shape[1]
    rng = DIL_RANGE
    blk = BAND_BLOCK
    assert seq % rng == 0 and all(rng % (d * blk) == 0 for d in dilations)
    qkv_r = qkv.reshape(batch, seq, width)

    def spec(g, which):
        return pl.BlockSpec((None, rng, HEAD_DIM),
                            lambda b, h, i: (b, i, (g * 3 + which) * A_HEADS + h))

    specs = [spec(g, which) for g in range(len(dilations)) for which in range(3)]
    kv_scratch = [pltpu.VMEM((rng + blk * d, HEAD_DIM), BF16) for d in dilations for _ in range(2)]
    out = pl.pallas_call(
        functools.partial(_dil_attn_kernel, rng=rng, dilations=tuple(dilations)),
        grid=(batch, A_HEADS, seq // rng),
        in_specs=specs,
        out_specs=pl.BlockSpec((None, rng, HEAD_DIM), lambda b, h, i: (b, i, h)),
        out_shape=jax.ShapeDtypeStruct((batch, seq, A_GROUP_WIDTH), BF16),
        scratch_shapes=[pltpu.VMEM((rng, HEAD_DIM), BF16)]
                       + [pltpu.VMEM((rng, HEAD_DIM), F32)] * 6 + kv_scratch,
        compiler_params=_params(("arbitrary", "arbitrary", "arbitrary")),
        name="dilated_attention",
    )(*([qkv_r] * len(specs)))
    return out.reshape(batch * seq, A_GROUP_WIDTH)


def _proj_res_kernel(a_ref, w_ref, x_ref, gate_ref, o_ref):
    y = jnp.dot(a_ref[...], w_ref[...], preferred_element_type=F32)
    o_ref[...] = x_ref[...] + gate_ref[...] * y


def _proj_residual(a, w, layer, x, gate, *, seq):
    t, k = a.shape
    d = w.shape[2]
    tm = _tile(seq, 1024)
    tn = _tile(d, 1024)
    return pl.pallas_call(
        _proj_res_kernel,
        grid=(t // tm, d // tn),
        in_specs=[
            pl.BlockSpec((tm, k), lambda i, j: (i, 0)),
            pl.BlockSpec((None, k, tn), lambda i, j: (layer, 0, j)),
            pl.BlockSpec((tm, tn), lambda i, j: (i, j)),
            pl.BlockSpec((None, 1, tn), lambda i, j: ((i * tm) // seq, 0, j)),
        ],
        out_specs=pl.BlockSpec((tm, tn), lambda i, j: (i, j)),
        out_shape=jax.ShapeDtypeStruct((t, d), F32),
        compiler_params=_params(("arbitrary", "arbitrary")),
        name="proj_residual",
    )(a, w, x, gate)


def _mlp_kernel(x_ref, g_ref, sh_ref, sc_ref, gate_ref, w1_ref, w2_ref, o_ref, h_scr):
    f = pl.program_id(1)

    @pl.when(f == 0)
    def _():
        _norm_mod_to(h_scr, x_ref, g_ref, sh_ref, sc_ref)
        o_ref[...] = jnp.zeros_like(o_ref)

    u = jnp.dot(h_scr[...], w1_ref[...], preferred_element_type=F32)
    u = jnp.square(jnp.maximum(u, 0.0)).astype(BF16)
    o_ref[...] += jnp.dot(u, w2_ref[...], preferred_element_type=F32)

    @pl.when(f == pl.num_programs(1) - 1)
    def _():
        o_ref[...] = x_ref[...] + gate_ref[...] * o_ref[...]


def _mlp(x, g, shift, scale, gate, w1, w2, layer, *, seq):
    t, d = x.shape
    ff = w1.shape[2]
    tm = _tile(seq, 1024)
    tf = _tile(ff, 512)
    vec = pl.BlockSpec((None, 1, d), lambda i, f: ((i * tm) // seq, 0, 0))
    return pl.pallas_call(
        _mlp_kernel,
        grid=(t // tm, ff // tf),
        in_specs=[
            pl.BlockSpec((tm, d), lambda i, f: (i, 0), pipeline_mode=pl.Buffered(1)),
            pl.BlockSpec((1, d), lambda i, f: (0, 0)),
            vec, vec, vec,
            pl.BlockSpec((None, d, tf), lambda i, f: (layer, 0, f)),
            pl.BlockSpec((None, tf, d), lambda i, f: (layer, f, 0)),
        ],
        out_specs=pl.BlockSpec((tm, d), lambda i, f: (i, 0)),
        out_shape=jax.ShapeDtypeStruct((t, d), F32),
        scratch_shapes=[pltpu.VMEM((tm, d), BF16)],
        compiler_params=_params(("arbitrary", "arbitrary")),
        name="sq_relu_mlp",
    )(x, g, shift, scale, gate, w1, w2)


def _diff_kernel(q_ref, k_ref, v_ref, lam_ref, g_ref, o_ref,
                 acc_scr, m_scr, l_scr, s_a, s_b, mx_a, mx_b, *, tq, lam_init):
    qi = pl.program_id(2)
    acc_scr[...] = jnp.zeros_like(acc_scr)
    m_scr[...] = jnp.full_like(m_scr, NEG_INF)
    l_scr[...] = jnp.zeros_like(l_scr)
    row = lax.broadcasted_iota(jnp.int32, (tq, tq), 0)
    col = lax.broadcasted_iota(jnp.int32, (tq, tq), 1)
    causal = col <= row

    def scores(j, s_ref, mx_ref, masked=False):
        k0 = pl.multiple_of(j * tq, tq)
        for c in range(2):
            cols = slice(c * HEAD_DIM, (c + 1) * HEAD_DIM)
            s = lax.dot_general(q_ref[:, cols], k_ref[pl.ds(k0, tq), cols],
                                (((1,), (1,)), ((), ())), preferred_element_type=F32)
            if masked:
                s = jnp.where(causal, s, NEG_INF)
            s_ref[c] = s
            mx_ref[c] = jnp.broadcast_to(jnp.max(s, axis=-1, keepdims=True), (tq, LANES))

    def update(j, s_ref, mx_ref):
        k0 = pl.multiple_of(j * tq, tq)
        vblk = v_ref[pl.ds(k0, tq), :]
        for c in range(2):
            m_prev = m_scr[c]
            m_new = jnp.maximum(m_prev, mx_ref[c])
            alpha = jnp.exp2(m_prev - m_new)
            p = jnp.exp2(s_ref[c] - jnp.tile(m_new, (1, tq // LANES)))
            row_sum = jnp.broadcast_to(jnp.sum(p, axis=-1, keepdims=True), (tq, LANES))
            l_scr[c] = alpha * l_scr[c] + row_sum
            m_scr[c] = m_new
            acc_scr[c] = (jnp.tile(alpha, (1, B_V_DIM // LANES)) * acc_scr[c]
                          + jnp.dot(p.astype(BF16), vblk, preferred_element_type=F32))

    @pl.when(qi > 0)
    def _():
        scores(0, s_a, mx_a)

    def pair(t, carry):
        j = 2 * t
        scores(j + 1, s_b, mx_b)
        update(j, s_a, mx_a)
        scores(j + 2, s_a, mx_a)
        update(j + 1, s_b, mx_b)
        return carry

    lax.fori_loop(0, jnp.maximum(qi - 1, 0) // 2, pair, 0)

    @pl.when(qi == 0)
    def _():
        scores(qi, s_a, mx_a, masked=True)
        update(qi, s_a, mx_a)

    @pl.when(qi % 2 == 1)
    def _():
        scores(qi, s_b, mx_b, masked=True)
        update(qi - 1, s_a, mx_a)
        update(qi, s_b, mx_b)

    @pl.when(jnp.logical_and(qi > 0, qi % 2 == 0))
    def _():
        scores(qi - 1, s_b, mx_b)
        update(qi - 2, s_a, mx_a)
        scores(qi, s_a, mx_a, masked=True)
        update(qi - 1, s_b, mx_b)
        update(qi, s_a, mx_a)

    lp = lam_ref[...]
    lam = (jnp.exp(jnp.sum(lp[0:1] * lp[1:2], axis=-1, keepdims=True))
           - jnp.exp(jnp.sum(lp[2:3] * lp[3:4], axis=-1, keepdims=True)) + lam_init)
    reps = (1, B_V_DIM // LANES)
    o = acc_scr[0] / jnp.tile(l_scr[0], reps) - lam * (acc_scr[1] / jnp.tile(l_scr[1], reps))
    ms = jnp.mean(o * o, axis=-1, keepdims=True)
    o = o * lax.rsqrt(ms + NORM_EPS) * g_ref[...] * (1.0 - lam_init)
    o_ref[...] = o.astype(o_ref.dtype)


def _diff_attention(q, kv, lam_params, subln_g, lam_init, *, batch, seq):
    t, qw = q.shape
    heads = qw // B_V_DIM
    tq = _tile(seq, 512)
    q_r = q.reshape(batch, seq, qw)
    kv_r = kv.reshape(batch, seq, 2 * qw)
    out = pl.pallas_call(
        functools.partial(_diff_kernel, tq=tq, lam_init=lam_init),
        grid=(batch, heads, seq // tq),
        in_specs=[
            pl.BlockSpec((None, tq, B_V_DIM), lambda b, h, i: (b, i, h)),
            pl.BlockSpec((None, seq, B_V_DIM), lambda b, h, i: (b, 0, h)),
            pl.BlockSpec((None, seq, B_V_DIM), lambda b, h, i: (b, 0, heads + h)),
            pl.BlockSpec(lam_params.shape, lambda b, h, i: (0, 0)),
            pl.BlockSpec((1, B_V_DIM), lambda b, h, i: (0, 0)),
        ],
        out_specs=pl.BlockSpec((None, tq, B_V_DIM), lambda b, h, i: (b, i, h)),
        out_shape=jax.ShapeDtypeStruct((batch, seq, qw), BF16),
        scratch_shapes=[pltpu.VMEM((2, tq, B_V_DIM), F32),
                        pltpu.VMEM((2, tq, LANES), F32), pltpu.VMEM((2, tq, LANES), F32),
                        pltpu.VMEM((2, tq, tq), F32), pltpu.VMEM((2, tq, tq), F32),
                        pltpu.VMEM((2, tq, LANES), F32), pltpu.VMEM((2, tq, LANES), F32)],
        compiler_params=_params(("arbitrary", "arbitrary", "arbitrary")),
        name="diff_attention",
    )(q_r, kv_r, kv_r, lam_params.astype(F32), subln_g.reshape(1, B_V_DIM))
    return out.reshape(t, qw)


def _rope_tables(positions):
    inv_freq = ROPE_THETA ** (-jnp.arange(0, ROT_DIM, 2, dtype=F32) / ROT_DIM)
    ang = positions.astype(F32).reshape(-1, 1) * inv_freq
    cos, sin = jnp.cos(ang), jnp.sin(ang)
    rest = HEAD_DIM - ROT_DIM
    cosf = jnp.concatenate([cos, cos, jnp.ones((ang.shape[0], rest), F32)], axis=-1)
    sinf = jnp.concatenate([-sin, sin, jnp.zeros((ang.shape[0], rest), F32)], axis=-1)
    return cosf, sinf


def kernel(x, c, positions, ada_w, ada_b, norm_g, a_w_qkv, a_q_gain, a_k_gain, a_w_o,
           kv_ada_w, kv_ada_b, kv_norm_g, kv_w_k, kv_w_v, kv_k_gain, b_w_q, b_q_gain,
           b_lambda, b_subln_g, b_w_o, mlp_w1, mlp_w2):
    batch, seq, d = x.shape
    depth = ada_w.shape[0]
    n_a = a_w_qkv.shape[0]
    t = batch * seq
    qk_scale = HEAD_DIM ** -0.5 * LOG2E

    cosf, sinf = _rope_tables(positions)
    c_pad = jnp.pad(c, ((0, 8 - batch % 8 if batch % 8 else 0), (0, 0)))
    mod = _modulation(c_pad, ada_w, ada_b)[:, :batch]
    kv_mod = _modulation(c_pad, kv_ada_w[None], kv_ada_b[None])[0, :batch]

    def vecs(m, n):
        return [v.reshape(batch, 1, d) for v in jnp.split(m, n, axis=-1)]

    w_qkv, w_ao = a_w_qkv.astype(BF16), a_w_o.astype(BF16)
    w_bq, w_bo = b_w_q.astype(BF16), b_w_o.astype(BF16)
    w1, w2 = mlp_w1.astype(BF16), mlp_w2.astype(BF16)
    xf = x.reshape(t, d)
    kv = None
    for layer in range(depth):
        if layer == n_a:
            shift, scale = vecs(kv_mod, 2)
            w_kv = jnp.concatenate([kv_w_k, kv_w_v], axis=1).astype(BF16)[None]
            n_kv = w_kv.shape[2]
            tn = _tile(n_kv // 2, 1024)
            k_gain_row = jnp.tile(kv_k_gain.reshape(-1), tn // (2 * HEAD_DIM))
            gains = jnp.broadcast_to(k_gain_row, (n_kv // tn, 1, tn))
            kv = _norm_mod_matmul(xf, kv_norm_g.reshape(1, d), shift, scale, w_kv, 0, gains,
                                  cosf, sinf, seq=seq, period=n_kv // tn, n_on=n_kv // (2 * tn),
                                  out_dtype=BF16)
        shift_t, scale_t, gate_t, shift_m, scale_m, gate_m = vecs(mod[layer], N_MOD)
        g_t = norm_g[layer, 0].reshape(1, d)
        if layer < n_a:
            n_groups = len(A_PATTERNS)
            ones = jnp.ones((HEAD_DIM,), F32)
            gains = jnp.stack([jnp.tile(v, A_HEADS) for g in range(n_groups)
                               for v in (a_q_gain[layer, g] * qk_scale, a_k_gain[layer, g], ones)])
            qkv = _norm_mod_matmul(xf, g_t, shift_t, scale_t, w_qkv, layer,
                                   gains.reshape(3 * n_groups, 1, A_GROUP_WIDTH),
                                   cosf, sinf, seq=seq, period=3, n_on=2, out_dtype=F32)
            mixed = _dilated_attention(qkv, [dil for _, dil in A_PATTERNS], batch, seq)
            xf = _proj_residual(mixed, w_ao, layer, xf, gate_t, seq=seq)
        else:
            j = layer - n_a
            lam_init = 0.8 - 0.6 * math.exp(-0.3 * layer)
            n_q = w_bq.shape[2]
            tn = _tile(n_q, 1024)
            q_gain_row = jnp.tile(b_q_gain[j].reshape(-1) * qk_scale, tn // (2 * HEAD_DIM))
            gains = jnp.broadcast_to(q_gain_row, (n_q // tn, 1, tn))
            q = _norm_mod_matmul(xf, g_t, shift_t, scale_t, w_bq, j, gains, cosf, sinf,
                                 seq=seq, period=1, n_on=1, out_dtype=BF16)
            o = _diff_attention(q, kv, b_lambda[j], b_subln_g[j], lam_init, batch=batch, seq=seq)
            xf = _proj_residual(o, w_bo, j, xf, gate_t, seq=seq)
        xf = _mlp(xf, norm_g[layer, 1].reshape(1, d), shift_m, scale_m, gate_m,
                  w1, w2, layer, seq=seq)
    return xf.reshape(batch, seq, d)
```

```python
import functools
import math

import jax
import jax.numpy as jnp
from jax import lax
from jax.experimental import pallas as pl
from jax.experimental.pallas import tpu as pltpu

F32 = jnp.float32
BF16 = jnp.bfloat16

HEAD_DIM = 128
ROT_DIM = HEAD_DIM // 4
ROPE_THETA = 500000.0
NORM_EPS = 1e-6
NEG_INF = -1e30
A_PATTERNS = ((128, 1), (512, 4), (2048, 16))
A_HEADS = 8
A_GROUP_WIDTH = A_HEADS * HEAD_DIM
BAND_BLOCK = 128
B_HEADS = 8
B_V_DIM = 2 * HEAD_DIM
N_MOD = 6

V7X_VMEM_LIMIT_BYTES = 56 * 1024 * 1024
LANES = 128
PROJ_ROW_SUB = 256
NORM_ROW_SUB = 128
DIL_RANGE = 2048
DIL_UNROLL = 16
LOG2E = math.log2(math.e)


def _params(semantics):
    return pltpu.CompilerParams(dimension_semantics=semantics,
                                vmem_limit_bytes=V7X_VMEM_LIMIT_BYTES)


def _tile(n, target):
    t = min(n, target)
    while n % t:
        t //= 2
    return t


def _mod_kernel(c_ref, w_ref, b_ref, o_ref):
    c = c_ref[...]
    c_act = (c * jax.nn.sigmoid(c)).astype(BF16)
    o_ref[...] = jnp.dot(c_act, w_ref[...].astype(BF16),
                         preferred_element_type=F32) + b_ref[...]


def _modulation(c_pad, w, b):
    n_layers, d, n = w.shape
    rows = c_pad.shape[0]
    tn = _tile(n, 1024)
    return pl.pallas_call(
        _mod_kernel,
        grid=(n_layers, n // tn),
        in_specs=[
            pl.BlockSpec((rows, d), lambda l, j: (0, 0)),
            pl.BlockSpec((None, d, tn), lambda l, j: (l, 0, j)),
            pl.BlockSpec((None, 1, tn), lambda l, j: (l, 0, j)),
        ],
        out_specs=pl.BlockSpec((None, rows, tn), lambda l, j: (l, 0, j)),
        out_shape=jax.ShapeDtypeStruct((n_layers, rows, n), F32),
        compiler_params=_params(("arbitrary", "arbitrary")),
        name="adaln_modulation",
    )(c_pad, w, b.reshape(n_layers, 1, n))


def _norm_mod_to(h_scr, x_ref, g_ref, sh_ref, sc_ref, zero):
    shift = sh_ref[...]
    gain = g_ref[...] * (1.0 + sc_ref[...])
    tm = x_ref.shape[0]
    sub = min(tm, NORM_ROW_SUB)
    rinv = []
    for rb in range(tm // sub):
        x = x_ref[rb * sub:(rb + 1) * sub, :]
        rinv.append(lax.rsqrt(jnp.mean(x * x, axis=-1, keepdims=True) + NORM_EPS))
    for rb in range(tm // sub):
        x = x_ref[pl.ds(pl.multiple_of(zero + rb * sub, 8), sub), :]
        h_scr[rb * sub:(rb + 1) * sub, :] = (x * rinv[rb] * gain + shift).astype(BF16)


def _qk_norm_rope(a, gain, cosf, sinf):
    half = ROT_DIM // 2
    ms = jnp.mean(a * a, axis=-1, keepdims=True)
    y = a * lax.rsqrt(ms + NORM_EPS) * gain
    lane = lax.broadcasted_iota(jnp.int32, y.shape, 1)
    partner = jnp.where(lane < half,
                        pltpu.roll(y, HEAD_DIM - half, 1),
                        pltpu.roll(y, half, 1))
    return y * cosf + partner * sinf


def _nmm_kernel(x_ref, g_ref, sh_ref, sc_ref, w_ref, gain_ref, cos_ref, sin_ref,
                o_ref, h_scr, *, period, n_on):
    j = pl.program_id(1)

    @pl.when(j == 0)
    def _():
        _norm_mod_to(h_scr, x_ref, g_ref, sh_ref, sc_ref, j * 8)

    qk_tile = lax.rem(j, period) < n_on
    tm, tn = o_ref.shape
    sub = min(tm, PROJ_ROW_SUB)

    @pl.when(qk_tile)
    def _():
        for rb in range(tm // sub):
            rows = slice(rb * sub, (rb + 1) * sub)
            acc = jnp.dot(h_scr[rows, :], w_ref[...], preferred_element_type=F32)
            cosf = cos_ref[rows, :]
            sinf = sin_ref[rows, :]
            for c in range(tn // HEAD_DIM):
                cols = slice(c * HEAD_DIM, (c + 1) * HEAD_DIM)
                o_ref[rows, cols] = _qk_norm_rope(acc[:, cols], gain_ref[:, cols],
                                                  cosf, sinf).astype(o_ref.dtype)

    @pl.when(jnp.logical_not(qk_tile))
    def _():
        for rb in range(tm // sub):
            rows = slice(rb * sub, (rb + 1) * sub)
            o_ref[rows, :] = jnp.dot(h_scr[rows, :], w_ref[...],
                                     preferred_element_type=F32).astype(o_ref.dtype)


def _norm_mod_matmul(x, g, shift, scale, w, layer, gains, cosf, sinf, *, seq, period, n_on,
                     out_dtype):
    t, d = x.shape
    n = w.shape[2]
    tm = _tile(seq, 1024)
    tn = gains.shape[2]
    kern = functools.partial(_nmm_kernel, period=period, n_on=n_on)
    return pl.pallas_call(
        kern,
        grid=(t // tm, n // tn),
        in_specs=[
            pl.BlockSpec((tm, d), lambda i, j: (i, 0)),
            pl.BlockSpec((1, d), lambda i, j: (0, 0)),
            pl.BlockSpec((None, 1, d), lambda i, j: ((i * tm) // seq, 0, 0)),
            pl.BlockSpec((None, 1, d), lambda i, j: ((i * tm) // seq, 0, 0)),
            pl.BlockSpec((None, d, tn), lambda i, j: (layer, 0, j)),
            pl.BlockSpec((None, 1, tn), lambda i, j: (j, 0, 0)),
            pl.BlockSpec((tm, LANES), lambda i, j: (i, 0)),
            pl.BlockSpec((tm, LANES), lambda i, j: (i, 0)),
        ],
        out_specs=pl.BlockSpec((tm, tn), lambda i, j: (i, j)),
        out_shape=jax.ShapeDtypeStruct((t, n), out_dtype),
        scratch_shapes=[pltpu.VMEM((tm, d), BF16)],
        compiler_params=_params(("arbitrary", "arbitrary")),
        name="norm_mod_proj",
    )(x, g, shift, scale, w, gains, cosf, sinf)


def _dil_attn_kernel(*refs, rng, dilations):
    n_g = len(dilations)
    ins, o_ref = refs[:3 * n_g], refs[3 * n_g]
    scratch = refs[3 * n_g + 1:]
    qd, og, mg, lg, oacc, macc, lacc = scratch[:7]
    kv_scr = scratch[7:]
    blk = BAND_BLOCK
    first_range = pl.program_id(2) == 0

    qi = lax.broadcasted_iota(jnp.int32, (blk, 2 * blk), 0)
    ki = lax.broadcasted_iota(jnp.int32, (blk, 2 * blk), 1)
    band = jnp.where((ki >= qi) & (ki <= qi + blk), 0.0, NEG_INF).astype(F32)
    no_prev = jnp.where(ki < blk, NEG_INF, 0.0).astype(F32)

    for g, d in enumerate(dilations):
        q_ref, k_ref, v_ref = ins[3 * g:3 * g + 3]
        kd, vd = kv_scr[2 * g:2 * g + 2]
        lq = rng // d
        res_rows = blk + lq
        per_res = lq // blk

        @pl.when(first_range)
        def _(kd=kd, vd=vd, d=d, res_rows=res_rows):
            for r in range(d):
                kd[r * res_rows:r * res_rows + blk, :] = jnp.zeros((blk, HEAD_DIM), BF16)
                vd[r * res_rows:r * res_rows + blk, :] = jnp.zeros((blk, HEAD_DIM), BF16)

        def take(ref, r, n, d=d):
            rows = ref[...] if d == 1 else ref[pl.ds(r, n, stride=d), :]
            return rows.astype(BF16)

        for r in range(d):
            qd[r * lq:(r + 1) * lq, :] = take(q_ref, r, lq)
            kd[r * res_rows + blk:(r + 1) * res_rows, :] = take(k_ref, r, lq)
            vd[r * res_rows + blk:(r + 1) * res_rows, :] = take(v_ref, r, lq)

        o_dst, m_dst, l_dst = (oacc, macc, lacc) if g == 0 else (og, mg, lg)

        def unit(u, carry, d=d, per_res=per_res, res_rows=res_rows, kd=kd, vd=vd,
                 o_dst=o_dst, m_dst=m_dst, l_dst=l_dst):
            shift = per_res.bit_length() - 1
            r = lax.shift_right_logical(u, shift)
            i = lax.bitwise_and(u, per_res - 1)
            q = qd[pl.ds(pl.multiple_of(u * blk, blk), blk), :]
            k0 = pl.multiple_of(r * res_rows + i * blk, blk)
            kw = kd[pl.ds(k0, 2 * blk), :]
            vw = vd[pl.ds(k0, 2 * blk), :]
            first = jnp.logical_and(first_range, i == 0).astype(F32)
            s = lax.dot_general(q, kw, (((1,), (1,)), ((), ())),
                                preferred_element_type=F32) + (band + first * no_prev)
            m = jnp.max(s, axis=-1, keepdims=True)
            p = jnp.exp2(s - m)
            den = jnp.sum(p, axis=-1, keepdims=True)
            acc = jnp.dot(p.astype(BF16), vw, preferred_element_type=F32)
            if d == 1:
                dst = pl.ds(pl.multiple_of(u * blk, blk), blk)
            else:
                dst = pl.ds(r + i * (blk * d), blk, stride=d)
            o_dst[dst, :] = acc
            m_dst[dst, :] = jnp.broadcast_to(m, (blk, HEAD_DIM))
            l_dst[dst, :] = jnp.broadcast_to(den, (blk, HEAD_DIM))
            return carry

        lax.fori_loop(0, rng // blk, unit, 0, unroll=DIL_UNROLL)

        for r in range(d):
            kd[r * res_rows:r * res_rows + blk, :] = kd[(r + 1) * res_rows - blk:(r + 1) * res_rows, :]
            vd[r * res_rows:r * res_rows + blk, :] = vd[(r + 1) * res_rows - blk:(r + 1) * res_rows, :]

        if g > 0:
            m_old, m_grp = macc[...], mg[...]
            m_new = jnp.maximum(m_old, m_grp)
            a_old, a_grp = jnp.exp2(m_old - m_new), jnp.exp2(m_grp - m_new)
            o_new = a_old * oacc[...] + a_grp * og[...]
            l_new = a_old * lacc[...] + a_grp * lg[...]
            if g == n_g - 1:
                o_ref[...] = (o_new / l_new).astype(o_ref.dtype)
            else:
                oacc[...], macc[...], lacc[...] = o_new, m_new, l_new

    if n_g == 1:
        o_ref[...] = (oacc[...] / lacc[...]).astype(o_ref.dtype)


def _dilated_attention(qkv, dilations, batch, seq):
    width = qkv.shape[1]
    rng = DIL_RANGE
    blk = BAND_BLOCK
    assert seq % rng == 0 and all(rng % (d * blk) == 0 for d in dilations)
    qkv_r = qkv.reshape(batch, seq, width)

    def spec(g, which):
        return pl.BlockSpec((None, rng, HEAD_DIM),
                            lambda b, h, i: (b, i, (g * 3 + which) * A_HEADS + h))

    specs = [spec(g, which) for g in range(len(dilations)) for which in range(3)]
    kv_scratch = [pltpu.VMEM((rng + blk * d, HEAD_DIM), BF16) for d in dilations for _ in range(2)]
    out = pl.pallas_call(
        functools.partial(_dil_attn_kernel, rng=rng, dilations=tuple(dilations)),
        grid=(batch, A_HEADS, seq // rng),
        in_specs=specs,
        out_specs=pl.BlockSpec((None, rng, HEAD_DIM), lambda b, h, i: (b, i, h)),
        out_shape=jax.ShapeDtypeStruct((batch, seq, A_GROUP_WIDTH), BF16),
        scratch_shapes=[pltpu.VMEM((rng, HEAD_DIM), BF16)]
                       + [pltpu.VMEM((rng, HEAD_DIM), F32)] * 6 + kv_scratch,
        compiler_params=_params(("arbitrary", "arbitrary", "arbitrary")),
        name="dilated_attention",
    )(*([qkv_r] * len(specs)))
    return out.reshape(batch * seq, A_GROUP_WIDTH)


def _proj_res_kernel(a_ref, w_ref, x_ref, gate_ref, o_ref):
    y = jnp.dot(a_ref[...], w_ref[...], preferred_element_type=F32)
    o_ref[...] = x_ref[...] + gate_ref[...] * y


def _proj_residual(a, w, layer, x, gate, *, seq):
    t, k = a.shape
    d = w.shape[2]
    tm = _tile(seq, 1024)
    tn = _tile(d, 1024)
    return pl.pallas_call(
        _proj_res_kernel,
        grid=(t // tm, d // tn),
        in_specs=[
            pl.BlockSpec((tm, k), lambda i, j: (i, 0)),
            pl.BlockSpec((None, k, tn), lambda i, j: (layer, 0, j)),
            pl.BlockSpec((tm, tn), lambda i, j: (i, j)),
            pl.BlockSpec((None, 1, tn), lambda i, j: ((i * tm) // seq, 0, j)),
        ],
        out_specs=pl.BlockSpec((tm, tn), lambda i, j: (i, j)),
        out_shape=jax.ShapeDtypeStruct((t, d), F32),
        compiler_params=_params(("arbitrary", "arbitrary")),
        name="proj_residual",
    )(a, w, x, gate)


def _mlp_kernel(x_ref, g_ref, sh_ref, sc_ref, gate_ref, w1_ref, w2_ref, o_ref, h_scr):
    f = pl.program_id(1)

    @pl.when(f == 0)
    def _():
        _norm_mod_to(h_scr, x_ref, g_ref, sh_ref, sc_ref, f * 8)
        o_ref[...] = jnp.zeros_like(o_ref)

    u = jnp.dot(h_scr[...], w1_ref[...], preferred_element_type=F32)
    u = jnp.square(jnp.maximum(u, 0.0)).astype(BF16)
    o_ref[...] += jnp.dot(u, w2_ref[...], preferred_element_type=F32)

    @pl.when(f == pl.num_programs(1) - 1)
    def _():
        o_ref[...] = x_ref[...] + gate_ref[...] * o_ref[...]


def _mlp(x, g, shift, scale, gate, w1, w2, layer, *, seq):
    t, d = x.shape
    ff = w1.shape[2]
    tm = _tile(seq, 1024)
    tf = _tile(ff, 512)
    vec = pl.BlockSpec((None, 1, d), lambda i, f: ((i * tm) // seq, 0, 0))
    return pl.pallas_call(
        _mlp_kernel,
        grid=(t // tm, ff // tf),
        in_specs=[
            pl.BlockSpec((tm, d), lambda i, f: (i, 0), pipeline_mode=pl.Buffered(1)),
            pl.BlockSpec((1, d), lambda i, f: (0, 0)),
            vec, vec, vec,
            pl.BlockSpec((None, d, tf), lambda i, f: (layer, 0, f)),
            pl.BlockSpec((None, tf, d), lambda i, f: (layer, f, 0)),
        ],
        out_specs=pl.BlockSpec((tm, d), lambda i, f: (i, 0)),
        out_shape=jax.ShapeDtypeStruct((t, d), F32),
        scratch_shapes=[pltpu.VMEM((tm, d), BF16)],
        compiler_params=_params(("arbitrary", "arbitrary")),
        name="sq_relu_mlp",
    )(x, g, shift, scale, gate, w1, w2)


def _diff_kernel(q_ref, k_ref, v_ref, lam_ref, g_ref, o_ref,
                 acc_scr, m_scr, l_scr, s_a, s_b, mx_a, mx_b, *, tq, lam_init):
    qi = pl.program_id(2)
    acc_scr[...] = jnp.zeros_like(acc_scr)
    m_scr[...] = jnp.full_like(m_scr, NEG_INF)
    l_scr[...] = jnp.zeros_like(l_scr)
    row = lax.broadcasted_iota(jnp.int32, (tq, tq), 0)
    col = lax.broadcasted_iota(jnp.int32, (tq, tq), 1)
    causal = col <= row

    def scores(j, s_ref, mx_ref, masked=False):
        k0 = pl.multiple_of(j * tq, tq)
        for c in range(2):
            cols = slice(c * HEAD_DIM, (c + 1) * HEAD_DIM)
            s = lax.dot_general(q_ref[:, cols], k_ref[pl.ds(k0, tq), cols],
                                (((1,), (1,)), ((), ())), preferred_element_type=F32)
            if masked:
                s = jnp.where(causal, s, NEG_INF)
            s_ref[c] = s
            mx_ref[c] = jnp.broadcast_to(jnp.max(s, axis=-1, keepdims=True), (tq, LANES))

    def update(j, s_ref, mx_ref):
        k0 = pl.multiple_of(j * tq, tq)
        vblk = v_ref[pl.ds(k0, tq), :]
        for c in range(2):
            m_prev = m_scr[c]
            m_new = jnp.maximum(m_prev, mx_ref[c])
            alpha = jnp.exp2(m_prev - m_new)
            p = jnp.exp2(s_ref[c] - jnp.tile(m_new, (1, tq // LANES)))
            row_sum = jnp.broadcast_to(jnp.sum(p, axis=-1, keepdims=True), (tq, LANES))
            l_scr[c] = alpha * l_scr[c] + row_sum
            m_scr[c] = m_new
            acc_scr[c] = (jnp.tile(alpha, (1, B_V_DIM // LANES)) * acc_scr[c]
                          + jnp.dot(p.astype(BF16), vblk, preferred_element_type=F32))

    @pl.when(qi > 0)
    def _():
        scores(0, s_a, mx_a)

    def pair(t, carry):
        j = 2 * t
        scores(j + 1, s_b, mx_b)
        update(j, s_a, mx_a)
        scores(j + 2, s_a, mx_a)
        update(j + 1, s_b, mx_b)
        return carry

    lax.fori_loop(0, jnp.maximum(qi - 1, 0) // 2, pair, 0)

    @pl.when(qi == 0)
    def _():
        scores(qi, s_a, mx_a, masked=True)
        update(qi, s_a, mx_a)

    @pl.when(qi % 2 == 1)
    def _():
        scores(qi, s_b, mx_b, masked=True)
        update(qi - 1, s_a, mx_a)
        update(qi, s_b, mx_b)

    @pl.when(jnp.logical_and(qi > 0, qi % 2 == 0))
    def _():
        scores(qi - 1, s_b, mx_b)
        update(qi - 2, s_a, mx_a)
        scores(qi, s_a, mx_a, masked=True)
        update(qi - 1, s_b, mx_b)
        update(qi, s_a, mx_a)

    lp = lam_ref[...]
    lam = (jnp.exp(jnp.sum(lp[0:1] * lp[1:2], axis=-1, keepdims=True))
           - jnp.exp(jnp.sum(lp[2:3] * lp[3:4], axis=-1, keepdims=True)) + lam_init)
    reps = (1, B_V_DIM // LANES)
    o = acc_scr[0] / jnp.tile(l_scr[0], reps) - lam * (acc_scr[1] / jnp.tile(l_scr[1], reps))
    ms = jnp.mean(o * o, axis=-1, keepdims=True)
    o = o * lax.rsqrt(ms + NORM_EPS) * g_ref[...] * (1.0 - lam_init)
    o_ref[...] = o.astype(o_ref.dtype)


def _diff_attention(q, kv, lam_params, subln_g, lam_init, *, batch, seq):
    t, qw = q.shape
    heads = qw // B_V_DIM
    tq = _tile(seq, 512)
    q_r = q.reshape(batch, seq, qw)
    kv_r = kv.reshape(batch, seq, 2 * qw)
    out = pl.pallas_call(
        functools.partial(_diff_kernel, tq=tq, lam_init=lam_init),
        grid=(batch, heads, seq // tq),
        in_specs=[
            pl.BlockSpec((None, tq, B_V_DIM), lambda b, h, i: (b, i, h)),
            pl.BlockSpec((None, seq, B_V_DIM), lambda b, h, i: (b, 0, h)),
            pl.BlockSpec((None, seq, B_V_DIM), lambda b, h, i: (b, 0, heads + h)),
            pl.BlockSpec(lam_params.shape, lambda b, h, i: (0, 0)),
            pl.BlockSpec((1, B_V_DIM), lambda b, h, i: (0, 0)),
        ],
        out_specs=pl.BlockSpec((None, tq, B_V_DIM), lambda b, h, i: (b, i, h)),
        out_shape=jax.ShapeDtypeStruct((batch, seq, qw), BF16),
        scratch_shapes=[pltpu.VMEM((2, tq, B_V_DIM), F32),
                        pltpu.VMEM((2, tq, LANES), F32), pltpu.VMEM((2, tq, LANES), F32),
                        pltpu.VMEM((2, tq, tq), F32), pltpu.VMEM((2, tq, tq), F32),
                        pltpu.VMEM((2, tq, LANES), F32), pltpu.VMEM((2, tq, LANES), F32)],
        compiler_params=_params(("arbitrary", "arbitrary", "arbitrary")),
        name="diff_attention",
    )(q_r, kv_r, kv_r, lam_params.astype(F32), subln_g.reshape(1, B_V_DIM))
    return out.reshape(t, qw)


def _rope_tables(positions):
    inv_freq = ROPE_THETA ** (-jnp.arange(0, ROT_DIM, 2, dtype=F32) / ROT_DIM)
    ang = positions.astype(F32).reshape(-1, 1) * inv_freq
    cos, sin = jnp.cos(ang), jnp.sin(ang)
    rest = HEAD_DIM - ROT_DIM
    cosf = jnp.concatenate([cos, cos, jnp.ones((ang.shape[0], rest), F32)], axis=-1)
    sinf = jnp.concatenate([-sin, sin, jnp.zeros((ang.shape[0], rest), F32)], axis=-1)
    return cosf, sinf


def kernel(x, c, positions, ada_w, ada_b, norm_g, a_w_qkv, a_q_gain, a_k_gain, a_w_o,
           kv_ada_w, kv_ada_b, kv_norm_g, kv_w_k, kv_w_v, kv_k_gain, b_w_q, b_q_gain,
           b_lambda, b_subln_g, b_w_o, mlp_w1, mlp_w2):
    batch, seq, d = x.shape
    depth = ada_w.shape[0]
    n_a = a_w_qkv.shape[0]
    t = batch * seq
    qk_scale = HEAD_DIM ** -0.5 * LOG2E

    cosf, sinf = _rope_tables(positions)
    c_pad = jnp.pad(c, ((0, 8 - batch % 8 if batch % 8 else 0), (0, 0)))
    mod = _modulation(c_pad, ada_w, ada_b)[:, :batch]
    kv_mod = _modulation(c_pad, kv_ada_w[None], kv_ada_b[None])[0, :batch]

    def vecs(m, n):
        return [v.reshape(batch, 1, d) for v in jnp.split(m, n, axis=-1)]

    w_qkv, w_ao = a_w_qkv.astype(BF16), a_w_o.astype(BF16)
    w_bq, w_bo = b_w_q.astype(BF16), b_w_o.astype(BF16)
    w1, w2 = mlp_w1.astype(BF16), mlp_w2.astype(BF16)
    xf = x.reshape(t, d)
    kv = None
    for layer in range(depth):
        if layer == n_a:
            shift, scale = vecs(kv_mod, 2)
            w_kv = jnp.concatenate([kv_w_k, kv_w_v], axis=1).astype(BF16)[None]
            n_kv = w_kv.shape[2]
            tn = _tile(n_kv // 2, 1024)
            k_gain_row = jnp.tile(kv_k_gain.reshape(-1), tn // (2 * HEAD_DIM))
            gains = jnp.broadcast_to(k_gain_row, (n_kv // tn, 1, tn))
            kv = _norm_mod_matmul(xf, kv_norm_g.reshape(1, d), shift, scale, w_kv, 0, gains,
                                  cosf, sinf, seq=seq, period=n_kv // tn, n_on=n_kv // (2 * tn),
                                  out_dtype=BF16)
        shift_t, scale_t, gate_t, shift_m, scale_m, gate_m = vecs(mod[layer], N_MOD)
        g_t = norm_g[layer, 0].reshape(1, d)
        if layer < n_a:
            n_groups = len(A_PATTERNS)
            ones = jnp.ones((HEAD_DIM,), F32)
            gains = jnp.stack([jnp.tile(v, A_HEADS) for g in range(n_groups)
                               for v in (a_q_gain[layer, g] * qk_scale, a_k_gain[layer, g], ones)])
            qkv = _norm_mod_matmul(xf, g_t, shift_t, scale_t, w_qkv, layer,
                                   gains.reshape(3 * n_groups, 1, A_GROUP_WIDTH),
                                   cosf, sinf, seq=seq, period=3, n_on=2, out_dtype=F32)
            mixed = _dilated_attention(qkv, [dil for _, dil in A_PATTERNS], batch, seq)
            xf = _proj_residual(mixed, w_ao, layer, xf, gate_t, seq=seq)
        else:
            j = layer - n_a
            lam_init = 0.8 - 0.6 * math.exp(-0.3 * layer)
            n_q = w_bq.shape[2]
            tn = _tile(n_q, 1024)
            q_gain_row = jnp.tile(b_q_gain[j].reshape(-1) * qk_scale, tn // (2 * HEAD_DIM))
            gains = jnp.broadcast_to(q_gain_row, (n_q // tn, 1, tn))
            q = _norm_mod_matmul(xf, g_t, shift_t, scale_t, w_bq, j, gains, cosf, sinf,
                                 seq=seq, period=1, n_on=1, out_dtype=BF16)
            o = _diff_attention(q, kv, b_lambda[j], b_subln_g[j], lam_init, batch=batch, seq=seq)
            xf = _proj_residual(o, w_bo, j, xf, gate_t, seq=seq)
        xf = _mlp(xf, norm_g[layer, 1].reshape(1, d), shift_m, scale_m, gate_m,
                  w1, w2, layer, seq=seq)
    return xf.reshape(batch, seq, d)
```

```python
import functools
import math

import jax
import jax.numpy as jnp
from jax import lax
from jax.experimental import pallas as pl
from jax.experimental.pallas import tpu as pltpu

F32 = jnp.float32
BF16 = jnp.bfloat16

HEAD_DIM = 128
ROT_DIM = HEAD_DIM // 4
ROPE_THETA = 500000.0
NORM_EPS = 1e-6
NEG_INF = -1e30
A_PATTERNS = ((128, 1), (512, 4), (2048, 16))
A_HEADS = 8
A_GROUP_WIDTH = A_HEADS * HEAD_DIM
BAND_BLOCK = 128
B_HEADS = 8
B_V_DIM = 2 * HEAD_DIM
N_MOD = 6

V7X_VMEM_LIMIT_BYTES = 56 * 1024 * 1024
LANES = 128
PROJ_ROW_SUB = 256
NORM_ROW_SUB = 128
DIL_RANGE = 2048
DIL_UNROLL = 16
LOG2E = math.log2(math.e)


def _params(semantics):
    return pltpu.CompilerParams(dimension_semantics=semantics,
                                vmem_limit_bytes=V7X_VMEM_LIMIT_BYTES)


def _tile(n, target):
    t = min(n, target)
    while n % t:
        t //= 2
    return t


def _mod_kernel(c_ref, w_ref, b_ref, o_ref):
    c = c_ref[...]
    c_act = (c * jax.nn.sigmoid(c)).astype(BF16)
    o_ref[...] = jnp.dot(c_act, w_ref[...].astype(BF16),
                         preferred_element_type=F32) + b_ref[...]


def _modulation(c_pad, w, b):
    n_layers, d, n = w.shape
    rows = c_pad.shape[0]
    tn = _tile(n, 1024)
    return pl.pallas_call(
        _mod_kernel,
        grid=(n_layers, n // tn),
        in_specs=[
            pl.BlockSpec((rows, d), lambda l, j: (0, 0)),
            pl.BlockSpec((None, d, tn), lambda l, j: (l, 0, j)),
            pl.BlockSpec((None, 1, tn), lambda l, j: (l, 0, j)),
        ],
        out_specs=pl.BlockSpec((None, rows, tn), lambda l, j: (l, 0, j)),
        out_shape=jax.ShapeDtypeStruct((n_layers, rows, n), F32),
        compiler_params=_params(("arbitrary", "arbitrary")),
        name="adaln_modulation",
    )(c_pad, w, b.reshape(n_layers, 1, n))


def _norm_mod_to(h_scr, x_ref, g_ref, sh_ref, sc_ref, zero):
    shift = sh_ref[...]
    gain = g_ref[...] * (1.0 + sc_ref[...])
    tm = x_ref.shape[0]
    sub = min(tm, NORM_ROW_SUB)
    rinv = []
    for rb in range(tm // sub):
        x = x_ref[rb * sub:(rb + 1) * sub, :]
        rinv.append(lax.rsqrt(jnp.mean(x * x, axis=-1, keepdims=True) + NORM_EPS))
    for rb in range(tm // sub):
        x = x_ref[pl.ds(pl.multiple_of(zero + rb * sub, 8), sub), :]
        h_scr[rb * sub:(rb + 1) * sub, :] = (x * rinv[rb] * gain + shift).astype(BF16)


def _qk_norm_rope(a, gain, cosf, sinf):
    half = ROT_DIM // 2
    ms = jnp.mean(a * a, axis=-1, keepdims=True)
    y = a * lax.rsqrt(ms + NORM_EPS) * gain
    lane = lax.broadcasted_iota(jnp.int32, y.shape, 1)
    partner = jnp.where(lane < half,
                        pltpu.roll(y, HEAD_DIM - half, 1),
                        pltpu.roll(y, half, 1))
    return y * cosf + partner * sinf


def _nmm_kernel(x_ref, g_ref, sh_ref, sc_ref, w_ref, gain_ref, cos_ref, sin_ref,
                o_ref, h_scr, *, period, n_on):
    j = pl.program_id(1)

    @pl.when(j == 0)
    def _():
        _norm_mod_to(h_scr, x_ref, g_ref, sh_ref, sc_ref, j * 8)

    qk_tile = lax.rem(j, period) < n_on
    tm, tn = o_ref.shape
    sub = min(tm, PROJ_ROW_SUB)

    @pl.when(qk_tile)
    def _():
        for rb in range(tm // sub):
            rows = slice(rb * sub, (rb + 1) * sub)
            acc = jnp.dot(h_scr[rows, :], w_ref[...], preferred_element_type=F32)
            cosf = cos_ref[rows, :]
            sinf = sin_ref[rows, :]
            for c in range(tn // HEAD_DIM):
                cols = slice(c * HEAD_DIM, (c + 1) * HEAD_DIM)
                o_ref[rows, cols] = _qk_norm_rope(acc[:, cols], gain_ref[:, cols],
                                                  cosf, sinf).astype(o_ref.dtype)

    @pl.when(jnp.logical_not(qk_tile))
    def _():
        for rb in range(tm // sub):
            rows = slice(rb * sub, (rb + 1) * sub)
            o_ref[rows, :] = jnp.dot(h_scr[rows, :], w_ref[...],
                                     preferred_element_type=F32).astype(o_ref.dtype)


def _norm_mod_matmul(x, g, shift, scale, w, layer, gains, cosf, sinf, *, seq, period, n_on,
                     out_dtype):
    t, d = x.shape
    n = w.shape[2]
    tm = _tile(seq, 1024)
    tn = gains.shape[2]
    kern = functools.partial(_nmm_kernel, period=period, n_on=n_on)
    return pl.pallas_call(
        kern,
        grid=(t // tm, n // tn),
        in_specs=[
            pl.BlockSpec((tm, d), lambda i, j: (i, 0)),
            pl.BlockSpec((1, d), lambda i, j: (0, 0)),
            pl.BlockSpec((None, 1, d), lambda i, j: ((i * tm) // seq, 0, 0)),
            pl.BlockSpec((None, 1, d), lambda i, j: ((i * tm) // seq, 0, 0)),
            pl.BlockSpec((None, d, tn), lambda i, j: (layer, 0, j)),
            pl.BlockSpec((None, 1, tn), lambda i, j: (j, 0, 0)),
            pl.BlockSpec((tm, LANES), lambda i, j: (i, 0)),
            pl.BlockSpec((tm, LANES), lambda i, j: (i, 0)),
        ],
        out_specs=pl.BlockSpec((tm, tn), lambda i, j: (i, j)),
        out_shape=jax.ShapeDtypeStruct((t, n), out_dtype),
        scratch_shapes=[pltpu.VMEM((tm, d), BF16)],
        compiler_params=_params(("arbitrary", "arbitrary")),
        name="norm_mod_proj",
    )(x, g, shift, scale, w, gains, cosf, sinf)


def _dil_attn_kernel(*refs, rng, dilations):
    n_g = len(dilations)
    ins, o_ref = refs[:3 * n_g], refs[3 * n_g]
    scratch = refs[3 * n_g + 1:]
    qd, og, mg, lg, oacc, macc, lacc = scratch[:7]
    kv_scr = scratch[7:]
    blk = BAND_BLOCK
    first_range = pl.program_id(2) == 0

    qi = lax.broadcasted_iota(jnp.int32, (blk, 2 * blk), 0)
    ki = lax.broadcasted_iota(jnp.int32, (blk, 2 * blk), 1)
    band = jnp.where((ki >= qi) & (ki <= qi + blk), 0.0, NEG_INF).astype(F32)
    no_prev = jnp.where(ki < blk, NEG_INF, 0.0).astype(F32)

    for g, d in enumerate(dilations):
        q_ref, k_ref, v_ref = ins[3 * g:3 * g + 3]
        kd, vd = kv_scr[2 * g:2 * g + 2]
        lq = rng // d
        res_rows = blk + lq
        per_res = lq // blk

        @pl.when(first_range)
        def _(kd=kd, vd=vd, d=d, res_rows=res_rows):
            for r in range(d):
                kd[r * res_rows:r * res_rows + blk, :] = jnp.zeros((blk, HEAD_DIM), BF16)
                vd[r * res_rows:r * res_rows + blk, :] = jnp.zeros((blk, HEAD_DIM), BF16)

        def take(ref, r, n, d=d):
            rows = ref[...] if d == 1 else ref[pl.ds(r, n, stride=d), :]
            return rows.astype(BF16)

        for r in range(d):
            qd[r * lq:(r + 1) * lq, :] = take(q_ref, r, lq)
            kd[r * res_rows + blk:(r + 1) * res_rows, :] = take(k_ref, r, lq)
            vd[r * res_rows + blk:(r + 1) * res_rows, :] = take(v_ref, r, lq)

        o_dst, m_dst, l_dst = (oacc, macc, lacc) if g == 0 else (og, mg, lg)

        def unit(u, carry, d=d, per_res=per_res, res_rows=res_rows, kd=kd, vd=vd,
                 o_dst=o_dst, m_dst=m_dst, l_dst=l_dst):
            shift = per_res.bit_length() - 1
            r = lax.shift_right_logical(u, shift)
            i = lax.bitwise_and(u, per_res - 1)
            q = qd[pl.ds(pl.multiple_of(u * blk, blk), blk), :]
            k0 = pl.multiple_of(r * res_rows + i * blk, blk)
            kw = kd[pl.ds(k0, 2 * blk), :]
            vw = vd[pl.ds(k0, 2 * blk), :]
            first = jnp.logical_and(first_range, i == 0).astype(F32)
            s = lax.dot_general(q, kw, (((1,), (1,)), ((), ())),
                                preferred_element_type=F32) + (band + first * no_prev)
            m = jnp.max(s, axis=-1, keepdims=True)
            p = jnp.exp2(s - m)
            den = jnp.sum(p, axis=-1, keepdims=True)
            acc = jnp.dot(p.astype(BF16), vw, preferred_element_type=F32)
            if d == 1:
                dst = pl.ds(pl.multiple_of(u * blk, blk), blk)
            else:
                dst = pl.ds(r + i * (blk * d), blk, stride=d)
            o_dst[dst, :] = acc
            m_dst[dst, :] = jnp.broadcast_to(m, (blk, HEAD_DIM))
            l_dst[dst, :] = jnp.broadcast_to(den, (blk, HEAD_DIM))
            return carry

        lax.fori_loop(0, rng // blk, unit, 0, unroll=DIL_UNROLL)

        for r in range(d):
            kd[r * res_rows:r * res_rows + blk, :] = kd[(r + 1) * res_rows - blk:(r + 1) * res_rows, :]
            vd[r * res_rows:r * res_rows + blk, :] = vd[(r + 1) * res_rows - blk:(r + 1) * res_rows, :]

        if g > 0:
            m_old, m_grp = macc[...], mg[...]
            m_new = jnp.maximum(m_old, m_grp)
            a_old, a_grp = jnp.exp2(m_old - m_new), jnp.exp2(m_grp - m_new)
            o_new = a_old * oacc[...] + a_grp * og[...]
            l_new = a_old * lacc[...] + a_grp * lg[...]
            if g == n_g - 1:
                o_ref[...] = (o_new / l_new).astype(o_ref.dtype)
            else:
                oacc[...], macc[...], lacc[...] = o_new, m_new, l_new

    if n_g == 1:
        o_ref[...] = (oacc[...] / lacc[...]).astype(o_ref.dtype)


def _dilated_attention(qkv, dilations, batch, seq):
    width = qkv.shape[1]
    rng = DIL_RANGE
    blk = BAND_BLOCK
    assert seq % rng == 0 and all(rng % (d * blk) == 0 for d in dilations)
    qkv_r = qkv.reshape(batch, seq, width)

    def spec(g, which):
        return pl.BlockSpec((None, rng, HEAD_DIM),
                            lambda b, h, i: (b, i, (g * 3 + which) * A_HEADS + h))

    specs = [spec(g, which) for g in range(len(dilations)) for which in range(3)]
    kv_scratch = [pltpu.VMEM((rng + blk * d, HEAD_DIM), BF16) for d in dilations for _ in range(2)]
    out = pl.pallas_call(
        functools.partial(_dil_attn_kernel, rng=rng, dilations=tuple(dilations)),
        grid=(batch, A_HEADS, seq // rng),
        in_specs=specs,
        out_specs=pl.BlockSpec((None, rng, HEAD_DIM), lambda b, h, i: (b, i, h)),
        out_shape=jax.ShapeDtypeStruct((batch, seq, A_GROUP_WIDTH), BF16),
        scratch_shapes=[pltpu.VMEM((rng, HEAD_DIM), BF16)]
                       + [pltpu.VMEM((rng, HEAD_DIM), F32)] * 6 + kv_scratch,
        compiler_params=_params(("arbitrary", "arbitrary", "arbitrary")),
        name="dilated_attention",
    )(*([qkv_r] * len(specs)))
    return out.reshape(batch * seq, A_GROUP_WIDTH)


def _proj_res_kernel(a_ref, w_ref, x_ref, gate_ref, o_ref):
    y = jnp.dot(a_ref[...], w_ref[...], preferred_element_type=F32)
    o_ref[...] = x_ref[...] + gate_ref[...] * y


def _proj_residual(a, w, layer, x, gate, *, seq):
    t, k = a.shape
    d = w.shape[2]
    tm = _tile(seq, 1024)
    tn = _tile(d, 1024)
    return pl.pallas_call(
        _proj_res_kernel,
        grid=(t // tm, d // tn),
        in_specs=[
            pl.BlockSpec((tm, k), lambda i, j: (i, 0)),
            pl.BlockSpec((None, k, tn), lambda i, j: (layer, 0, j)),
            pl.BlockSpec((tm, tn), lambda i, j: (i, j)),
            pl.BlockSpec((None, 1, tn), lambda i, j: ((i * tm) // seq, 0, j)),
        ],
        out_specs=pl.BlockSpec((tm, tn), lambda i, j: (i, j)),
        out_shape=jax.ShapeDtypeStruct((t, d), F32),
        compiler_params=_params(("arbitrary", "arbitrary")),
        name="proj_residual",
    )(a, w, x, gate)


def _mlp_kernel(*refs, fused_proj):
    if fused_proj:
        x_ref, a_ref, wo_ref, gate_t_ref = refs[:4]
    else:
        x_ref = refs[0]
    g_ref, sh_ref, sc_ref, gate_ref, w1_ref, w2_ref, o_ref, h_scr = refs[-8:]
    f = pl.program_id(1)

    @pl.when(f == 0)
    def _():
        if fused_proj:
            tm = x_ref.shape[0]
            sub = min(tm, PROJ_ROW_SUB)
            for rb in range(tm // sub):
                rows = slice(rb * sub, (rb + 1) * sub)
                y = jnp.dot(a_ref[rows, :], wo_ref[...], preferred_element_type=F32)
                o_ref[rows, :] = x_ref[rows, :] + gate_t_ref[...] * y
        else:
            o_ref[...] = x_ref[...]
        _norm_mod_to(h_scr, o_ref, g_ref, sh_ref, sc_ref, f * 8)

    u = jnp.dot(h_scr[...], w1_ref[...], preferred_element_type=F32)
    u = jnp.square(jnp.maximum(u, 0.0)).astype(BF16)
    o_ref[...] += gate_ref[...] * jnp.dot(u, w2_ref[...], preferred_element_type=F32)


def _mlp(x, g, shift, scale, gate, w1, w2, layer, *, seq, proj=None):
    t, d = x.shape
    ff = w1.shape[2]
    tm = _tile(seq, 1024)
    tf = _tile(ff, 512)
    vec = pl.BlockSpec((None, 1, d), lambda i, f: ((i * tm) // seq, 0, 0))
    once = dict(pipeline_mode=pl.Buffered(1))
    args = [x]
    specs = [pl.BlockSpec((tm, d), lambda i, f: (i, 0), **once)]
    if proj is not None:
        a, w_o, o_layer, gate_t = proj
        k = a.shape[1]
        args += [a, w_o, gate_t]
        specs += [pl.BlockSpec((tm, k), lambda i, f: (i, 0), **once),
                  pl.BlockSpec((None, k, d), lambda i, f: (o_layer, 0, 0), **once),
                  vec]
    args += [g, shift, scale, gate, w1, w2]
    specs += [pl.BlockSpec((1, d), lambda i, f: (0, 0)), vec, vec, vec,
              pl.BlockSpec((None, d, tf), lambda i, f: (layer, 0, f)),
              pl.BlockSpec((None, tf, d), lambda i, f: (layer, f, 0))]
    return pl.pallas_call(
        functools.partial(_mlp_kernel, fused_proj=proj is not None),
        grid=(t // tm, ff // tf),
        in_specs=specs,
        out_specs=pl.BlockSpec((tm, d), lambda i, f: (i, 0)),
        out_shape=jax.ShapeDtypeStruct((t, d), F32),
        scratch_shapes=[pltpu.VMEM((tm, d), BF16)],
        compiler_params=_params(("arbitrary", "arbitrary")),
        name="sq_relu_mlp",
    )(*args)


def _diff_kernel(q_ref, k_ref, v_ref, lam_ref, g_ref, o_ref,
                 acc_scr, m_scr, l_scr, s_a, s_b, mx_a, mx_b, *, tq, lam_init):
    qi = pl.program_id(2)
    acc_scr[...] = jnp.zeros_like(acc_scr)
    m_scr[...] = jnp.full_like(m_scr, NEG_INF)
    l_scr[...] = jnp.zeros_like(l_scr)
    row = lax.broadcasted_iota(jnp.int32, (tq, tq), 0)
    col = lax.broadcasted_iota(jnp.int32, (tq, tq), 1)
    causal = col <= row

    def scores(j, s_ref, mx_ref, masked=False):
        k0 = pl.multiple_of(j * tq, tq)
        for c in range(2):
            cols = slice(c * HEAD_DIM, (c + 1) * HEAD_DIM)
            s = lax.dot_general(q_ref[:, cols], k_ref[pl.ds(k0, tq), cols],
                                (((1,), (1,)), ((), ())), preferred_element_type=F32)
            if masked:
                s = jnp.where(causal, s, NEG_INF)
            s_ref[c] = s
            mx_ref[c] = jnp.broadcast_to(jnp.max(s, axis=-1, keepdims=True), (tq, LANES))

    def update(j, s_ref, mx_ref):
        k0 = pl.multiple_of(j * tq, tq)
        vblk = v_ref[pl.ds(k0, tq), :]
        for c in range(2):
            m_prev = m_scr[c]
            m_new = jnp.maximum(m_prev, mx_ref[c])
            alpha = jnp.exp2(m_prev - m_new)
            p = jnp.exp2(s_ref[c] - jnp.tile(m_new, (1, tq // LANES)))
            row_sum = jnp.broadcast_to(jnp.sum(p, axis=-1, keepdims=True), (tq, LANES))
            l_scr[c] = alpha * l_scr[c] + row_sum
            m_scr[c] = m_new
            acc_scr[c] = (jnp.tile(alpha, (1, B_V_DIM // LANES)) * acc_scr[c]
                          + jnp.dot(p.astype(BF16), vblk, preferred_element_type=F32))

    @pl.when(qi > 0)
    def _():
        scores(0, s_a, mx_a)

    def pair(t, carry):
        j = 2 * t
        scores(j + 1, s_b, mx_b)
        update(j, s_a, mx_a)
        scores(j + 2, s_a, mx_a)
        update(j + 1, s_b, mx_b)
        return carry

    lax.fori_loop(0, jnp.maximum(qi - 1, 0) // 2, pair, 0)

    @pl.when(qi == 0)
    def _():
        scores(qi, s_a, mx_a, masked=True)
        update(qi, s_a, mx_a)

    @pl.when(qi % 2 == 1)
    def _():
        scores(qi, s_b, mx_b, masked=True)
        update(qi - 1, s_a, mx_a)
        update(qi, s_b, mx_b)

    @pl.when(jnp.logical_and(qi > 0, qi % 2 == 0))
    def _():
        scores(qi - 1, s_b, mx_b)
        update(qi - 2, s_a, mx_a)
        scores(qi, s_a, mx_a, masked=True)
        update(qi - 1, s_b, mx_b)
        update(qi, s_a, mx_a)

    lp = lam_ref[...]
    lam = (jnp.exp(jnp.sum(lp[0:1] * lp[1:2], axis=-1, keepdims=True))
           - jnp.exp(jnp.sum(lp[2:3] * lp[3:4], axis=-1, keepdims=True)) + lam_init)
    reps = (1, B_V_DIM // LANES)
    o = acc_scr[0] / jnp.tile(l_scr[0], reps) - lam * (acc_scr[1] / jnp.tile(l_scr[1], reps))
    ms = jnp.mean(o * o, axis=-1, keepdims=True)
    o = o * lax.rsqrt(ms + NORM_EPS) * g_ref[...] * (1.0 - lam_init)
    o_ref[...] = o.astype(o_ref.dtype)


def _diff_attention(q, kv, lam_params, subln_g, lam_init, *, batch, seq):
    t, qw = q.shape
    heads = qw // B_V_DIM
    tq = _tile(seq, 512)
    q_r = q.reshape(batch, seq, qw)
    kv_r = kv.reshape(batch, seq, 2 * qw)
    out = pl.pallas_call(
        functools.partial(_diff_kernel, tq=tq, lam_init=lam_init),
        grid=(batch, heads, seq // tq),
        in_specs=[
            pl.BlockSpec((None, tq, B_V_DIM), lambda b, h, i: (b, i, h)),
            pl.BlockSpec((None, seq, B_V_DIM), lambda b, h, i: (b, 0, h)),
            pl.BlockSpec((None, seq, B_V_DIM), lambda b, h, i: (b, 0, heads + h)),
            pl.BlockSpec(lam_params.shape, lambda b, h, i: (0, 0)),
            pl.BlockSpec((1, B_V_DIM), lambda b, h, i: (0, 0)),
        ],
        out_specs=pl.BlockSpec((None, tq, B_V_DIM), lambda b, h, i: (b, i, h)),
        out_shape=jax.ShapeDtypeStruct((batch, seq, qw), BF16),
        scratch_shapes=[pltpu.VMEM((2, tq, B_V_DIM), F32),
                        pltpu.VMEM((2, tq, LANES), F32), pltpu.VMEM((2, tq, LANES), F32),
                        pltpu.VMEM((2, tq, tq), F32), pltpu.VMEM((2, tq, tq), F32),
                        pltpu.VMEM((2, tq, LANES), F32), pltpu.VMEM((2, tq, LANES), F32)],
        compiler_params=_params(("arbitrary", "arbitrary", "arbitrary")),
        name="diff_attention",
    )(q_r, kv_r, kv_r, lam_params.astype(F32), subln_g.reshape(1, B_V_DIM))
    return out.reshape(t, qw)


def _rope_tables(positions):
    inv_freq = ROPE_THETA ** (-jnp.arange(0, ROT_DIM, 2, dtype=F32) / ROT_DIM)
    ang = positions.astype(F32).reshape(-1, 1) * inv_freq
    cos, sin = jnp.cos(ang), jnp.sin(ang)
    rest = HEAD_DIM - ROT_DIM
    cosf = jnp.concatenate([cos, cos, jnp.ones((ang.shape[0], rest), F32)], axis=-1)
    sinf = jnp.concatenate([-sin, sin, jnp.zeros((ang.shape[0], rest), F32)], axis=-1)
    return cosf, sinf


def kernel(x, c, positions, ada_w, ada_b, norm_g, a_w_qkv, a_q_gain, a_k_gain, a_w_o,
           kv_ada_w, kv_ada_b, kv_norm_g, kv_w_k, kv_w_v, kv_k_gain, b_w_q, b_q_gain,
           b_lambda, b_subln_g, b_w_o, mlp_w1, mlp_w2):
    batch, seq, d = x.shape
    depth = ada_w.shape[0]
    n_a = a_w_qkv.shape[0]
    t = batch * seq
    qk_scale = HEAD_DIM ** -0.5 * LOG2E

    cosf, sinf = _rope_tables(positions)
    c_pad = jnp.pad(c, ((0, 8 - batch % 8 if batch % 8 else 0), (0, 0)))
    mod = _modulation(c_pad, ada_w, ada_b)[:, :batch]
    kv_mod = _modulation(c_pad, kv_ada_w[None], kv_ada_b[None])[0, :batch]

    def vecs(m, n):
        return [v.reshape(batch, 1, d) for v in jnp.split(m, n, axis=-1)]

    w_qkv, w_ao = a_w_qkv.astype(BF16), a_w_o.astype(BF16)
    w_bq, w_bo = b_w_q.astype(BF16), b_w_o.astype(BF16)
    w1, w2 = mlp_w1.astype(BF16), mlp_w2.astype(BF16)
    xf = x.reshape(t, d)
    kv = None
    for layer in range(depth):
        if layer == n_a:
            shift, scale = vecs(kv_mod, 2)
            w_kv = jnp.concatenate([kv_w_k, kv_w_v], axis=1).astype(BF16)[None]
            n_kv = w_kv.shape[2]
            tn = _tile(n_kv // 2, 1024)
            k_gain_row = jnp.tile(kv_k_gain.reshape(-1), tn // (2 * HEAD_DIM))
            gains = jnp.broadcast_to(k_gain_row, (n_kv // tn, 1, tn))
            kv = _norm_mod_matmul(xf, kv_norm_g.reshape(1, d), shift, scale, w_kv, 0, gains,
                                  cosf, sinf, seq=seq, period=n_kv // tn, n_on=n_kv // (2 * tn),
                                  out_dtype=BF16)
        shift_t, scale_t, gate_t, shift_m, scale_m, gate_m = vecs(mod[layer], N_MOD)
        g_t = norm_g[layer, 0].reshape(1, d)
        if layer < n_a:
            n_groups = len(A_PATTERNS)
            ones = jnp.ones((HEAD_DIM,), F32)
            gains = jnp.stack([jnp.tile(v, A_HEADS) for g in range(n_groups)
                               for v in (a_q_gain[layer, g] * qk_scale, a_k_gain[layer, g], ones)])
            qkv = _norm_mod_matmul(xf, g_t, shift_t, scale_t, w_qkv, layer,
                                   gains.reshape(3 * n_groups, 1, A_GROUP_WIDTH),
                                   cosf, sinf, seq=seq, period=3, n_on=2, out_dtype=F32)
            mixed = _dilated_attention(qkv, [dil for _, dil in A_PATTERNS], batch, seq)
            proj = (mixed, w_ao, layer, gate_t)
        else:
            j = layer - n_a
            lam_init = 0.8 - 0.6 * math.exp(-0.3 * layer)
            n_q = w_bq.shape[2]
            tn = _tile(n_q, 1024)
            q_gain_row = jnp.tile(b_q_gain[j].reshape(-1) * qk_scale, tn // (2 * HEAD_DIM))
            gains = jnp.broadcast_to(q_gain_row, (n_q // tn, 1, tn))
            q = _norm_mod_matmul(xf, g_t, shift_t, scale_t, w_bq, j, gains, cosf, sinf,
                                 seq=seq, period=1, n_on=1, out_dtype=BF16)
            o = _diff_attention(q, kv, b_lambda[j], b_subln_g[j], lam_init, batch=batch, seq=seq)
            xf = _proj_residual(o, w_bo, j, xf, gate_t, seq=seq)
            proj = None
        xf = _mlp(xf, norm_g[layer, 1].reshape(1, d), shift_m, scale_m, gate_m,
                  w1, w2, layer, seq=seq, proj=proj)
    return xf.reshape(batch, seq, d)
```

```python
import functools
import math

import jax
import jax.numpy as jnp
from jax import lax
from jax.experimental import pallas as pl
from jax.experimental.pallas import tpu as pltpu

F32 = jnp.float32
BF16 = jnp.bfloat16

HEAD_DIM = 128
ROT_DIM = HEAD_DIM // 4
ROPE_THETA = 500000.0
NORM_EPS = 1e-6
NEG_INF = -1e30
A_PATTERNS = ((128, 1), (512, 4), (2048, 16))
A_HEADS = 8
A_GROUP_WIDTH = A_HEADS * HEAD_DIM
BAND_BLOCK = 128
B_HEADS = 8
B_V_DIM = 2 * HEAD_DIM
N_MOD = 6

V7X_VMEM_LIMIT_BYTES = 56 * 1024 * 1024
LANES = 128
PROJ_ROW_SUB = 256
NORM_ROW_SUB = 128
DIL_RANGE = 2048
DIL_UNROLL = 16
LOG2E = math.log2(math.e)


def _params(semantics):
    return pltpu.CompilerParams(dimension_semantics=semantics,
                                vmem_limit_bytes=V7X_VMEM_LIMIT_BYTES)


def _tile(n, target):
    t = min(n, target)
    while n % t:
        t //= 2
    return t


def _mod_kernel(c_ref, w_ref, b_ref, o_ref):
    c = c_ref[...]
    c_act = (c * jax.nn.sigmoid(c)).astype(BF16)
    o_ref[...] = jnp.dot(c_act, w_ref[...].astype(BF16),
                         preferred_element_type=F32) + b_ref[...]


def _modulation(c_pad, w, b):
    n_layers, d, n = w.shape
    rows = c_pad.shape[0]
    tn = _tile(n, 1024)
    return pl.pallas_call(
        _mod_kernel,
        grid=(n_layers, n // tn),
        in_specs=[
            pl.BlockSpec((rows, d), lambda l, j: (0, 0)),
            pl.BlockSpec((None, d, tn), lambda l, j: (l, 0, j)),
            pl.BlockSpec((None, 1, tn), lambda l, j: (l, 0, j)),
        ],
        out_specs=pl.BlockSpec((None, rows, tn), lambda l, j: (l, 0, j)),
        out_shape=jax.ShapeDtypeStruct((n_layers, rows, n), F32),
        compiler_params=_params(("arbitrary", "arbitrary")),
        name="adaln_modulation",
    )(c_pad, w, b.reshape(n_layers, 1, n))


def _norm_mod_to(h_scr, x_ref, g_ref, sh_ref, sc_ref, zero):
    shift = sh_ref[...]
    gain = g_ref[...] * (1.0 + sc_ref[...])
    tm = x_ref.shape[0]
    sub = min(tm, NORM_ROW_SUB)
    rinv = []
    for rb in range(tm // sub):
        x = x_ref[rb * sub:(rb + 1) * sub, :]
        rinv.append(lax.rsqrt(jnp.mean(x * x, axis=-1, keepdims=True) + NORM_EPS))
    for rb in range(tm // sub):
        x = x_ref[pl.ds(pl.multiple_of(zero + rb * sub, 8), sub), :]
        h_scr[rb * sub:(rb + 1) * sub, :] = (x * rinv[rb] * gain + shift).astype(BF16)


def _qk_norm_rope(a, gain, cosf, sinf):
    half = ROT_DIM // 2
    ms = jnp.mean(a * a, axis=-1, keepdims=True)
    y = a * lax.rsqrt(ms + NORM_EPS) * gain
    lane = lax.broadcasted_iota(jnp.int32, y.shape, 1)
    partner = jnp.where(lane < half,
                        pltpu.roll(y, HEAD_DIM - half, 1),
                        pltpu.roll(y, half, 1))
    return y * cosf + partner * sinf


def _nmm_kernel(x_ref, g_ref, sh_ref, sc_ref, w_ref, gain_ref, cos_ref, sin_ref,
                o_ref, h_scr, *, period, n_on):
    j = pl.program_id(1)

    @pl.when(j == 0)
    def _():
        _norm_mod_to(h_scr, x_ref, g_ref, sh_ref, sc_ref, j * 8)

    qk_tile = lax.rem(j, period) < n_on
    tm, tn = o_ref.shape
    sub = min(tm, PROJ_ROW_SUB)

    @pl.when(qk_tile)
    def _():
        for rb in range(tm // sub):
            rows = slice(rb * sub, (rb + 1) * sub)
            acc = jnp.dot(h_scr[rows, :], w_ref[...], preferred_element_type=F32)
            cosf = cos_ref[rows, :]
            sinf = sin_ref[rows, :]
            for c in range(tn // HEAD_DIM):
                cols = slice(c * HEAD_DIM, (c + 1) * HEAD_DIM)
                o_ref[rows, cols] = _qk_norm_rope(acc[:, cols], gain_ref[:, cols],
                                                  cosf, sinf).astype(o_ref.dtype)

    @pl.when(jnp.logical_not(qk_tile))
    def _():
        for rb in range(tm // sub):
            rows = slice(rb * sub, (rb + 1) * sub)
            o_ref[rows, :] = jnp.dot(h_scr[rows, :], w_ref[...],
                                     preferred_element_type=F32).astype(o_ref.dtype)


def _norm_mod_matmul(x, g, shift, scale, w, layer, gains, cosf, sinf, *, seq, period, n_on,
                     out_dtype):
    t, d = x.shape
    n = w.shape[2]
    tm = _tile(seq, 1024)
    tn = gains.shape[2]
    kern = functools.partial(_nmm_kernel, period=period, n_on=n_on)
    return pl.pallas_call(
        kern,
        grid=(t // tm, n // tn),
        in_specs=[
            pl.BlockSpec((tm, d), lambda i, j: (i, 0)),
            pl.BlockSpec((1, d), lambda i, j: (0, 0)),
            pl.BlockSpec((None, 1, d), lambda i, j: ((i * tm) // seq, 0, 0)),
            pl.BlockSpec((None, 1, d), lambda i, j: ((i * tm) // seq, 0, 0)),
            pl.BlockSpec((None, d, tn), lambda i, j: (layer, 0, j)),
            pl.BlockSpec((None, 1, tn), lambda i, j: (j, 0, 0)),
            pl.BlockSpec((tm, LANES), lambda i, j: (i, 0)),
            pl.BlockSpec((tm, LANES), lambda i, j: (i, 0)),
        ],
        out_specs=pl.BlockSpec((tm, tn), lambda i, j: (i, j)),
        out_shape=jax.ShapeDtypeStruct((t, n), out_dtype),
        scratch_shapes=[pltpu.VMEM((tm, d), BF16)],
        compiler_params=_params(("arbitrary", "arbitrary")),
        name="norm_mod_proj",
    )(x, g, shift, scale, w, gains, cosf, sinf)


def _dil_attn_kernel(*refs, rng, dilations):
    n_g = len(dilations)
    ins, o_ref = refs[:3 * n_g], refs[3 * n_g]
    scratch = refs[3 * n_g + 1:]
    qd, og, mg, lg, oacc, macc, lacc = scratch[:7]
    kv_scr = scratch[7:]
    blk = BAND_BLOCK
    first_range = pl.program_id(2) == 0

    qi = lax.broadcasted_iota(jnp.int32, (blk, 2 * blk), 0)
    ki = lax.broadcasted_iota(jnp.int32, (blk, 2 * blk), 1)
    band = jnp.where((ki >= qi) & (ki <= qi + blk), 0.0, NEG_INF).astype(F32)
    no_prev = jnp.where(ki < blk, NEG_INF, 0.0).astype(F32)

    for g, d in enumerate(dilations):
        q_ref, k_ref, v_ref = ins[3 * g:3 * g + 3]
        kd, vd = kv_scr[2 * g:2 * g + 2]
        lq = rng // d
        res_rows = blk + lq
        per_res = lq // blk

        @pl.when(first_range)
        def _(kd=kd, vd=vd, d=d, res_rows=res_rows):
            for r in range(d):
                kd[r * res_rows:r * res_rows + blk, :] = jnp.zeros((blk, HEAD_DIM), BF16)
                vd[r * res_rows:r * res_rows + blk, :] = jnp.zeros((blk, HEAD_DIM), BF16)

        def take(ref, r, n, d=d):
            rows = ref[...] if d == 1 else ref[pl.ds(r, n, stride=d), :]
            return rows.astype(BF16)

        for r in range(d):
            qd[r * lq:(r + 1) * lq, :] = take(q_ref, r, lq)
            kd[r * res_rows + blk:(r + 1) * res_rows, :] = take(k_ref, r, lq)
            vd[r * res_rows + blk:(r + 1) * res_rows, :] = take(v_ref, r, lq)

        o_dst, m_dst, l_dst = (oacc, macc, lacc) if g == 0 else (og, mg, lg)

        def unit(u, carry, d=d, per_res=per_res, res_rows=res_rows, kd=kd, vd=vd,
                 o_dst=o_dst, m_dst=m_dst, l_dst=l_dst):
            shift = per_res.bit_length() - 1
            r = lax.shift_right_logical(u, shift)
            i = lax.bitwise_and(u, per_res - 1)
            q = qd[pl.ds(pl.multiple_of(u * blk, blk), blk), :]
            k0 = pl.multiple_of(r * res_rows + i * blk, blk)
            kw = kd[pl.ds(k0, 2 * blk), :]
            vw = vd[pl.ds(k0, 2 * blk), :]
            first = jnp.logical_and(first_range, i == 0).astype(F32)
            s = lax.dot_general(q, kw, (((1,), (1,)), ((), ())),
                                preferred_element_type=F32) + (band + first * no_prev)
            m = jnp.max(s, axis=-1, keepdims=True)
            p = jnp.exp2(s - m)
            den = jnp.sum(p, axis=-1, keepdims=True)
            acc = jnp.dot(p.astype(BF16), vw, preferred_element_type=F32)
            if d == 1:
                dst = pl.ds(pl.multiple_of(u * blk, blk), blk)
            else:
                dst = pl.ds(r + i * (blk * d), blk, stride=d)
            o_dst[dst, :] = acc
            m_dst[dst, :] = jnp.broadcast_to(m, (blk, HEAD_DIM))
            l_dst[dst, :] = jnp.broadcast_to(den, (blk, HEAD_DIM))
            return carry

        lax.fori_loop(0, rng // blk, unit, 0, unroll=DIL_UNROLL)

        for r in range(d):
            kd[r * res_rows:r * res_rows + blk, :] = kd[(r + 1) * res_rows - blk:(r + 1) * res_rows, :]
            vd[r * res_rows:r * res_rows + blk, :] = vd[(r + 1) * res_rows - blk:(r + 1) * res_rows, :]

        if g > 0:
            m_old, m_grp = macc[...], mg[...]
            m_new = jnp.maximum(m_old, m_grp)
            a_old, a_grp = jnp.exp2(m_old - m_new), jnp.exp2(m_grp - m_new)
            o_new = a_old * oacc[...] + a_grp * og[...]
            l_new = a_old * lacc[...] + a_grp * lg[...]
            if g == n_g - 1:
                o_ref[...] = (o_new / l_new).astype(o_ref.dtype)
            else:
                oacc[...], macc[...], lacc[...] = o_new, m_new, l_new

    if n_g == 1:
        o_ref[...] = (oacc[...] / lacc[...]).astype(o_ref.dtype)


def _dilated_attention(qkv, dilations, batch, seq):
    width = qkv.shape[1]
    rng = DIL_RANGE
    blk = BAND_BLOCK
    assert seq % rng == 0 and all(rng % (d * blk) == 0 for d in dilations)
    qkv_r = qkv.reshape(batch, seq, width)

    def spec(g, which):
        return pl.BlockSpec((None, rng, HEAD_DIM),
                            lambda b, h, i: (b, i, (g * 3 + which) * A_HEADS + h))

    specs = [spec(g, which) for g in range(len(dilations)) for which in range(3)]
    kv_scratch = [pltpu.VMEM((rng + blk * d, HEAD_DIM), BF16) for d in dilations for _ in range(2)]
    out = pl.pallas_call(
        functools.partial(_dil_attn_kernel, rng=rng, dilations=tuple(dilations)),
        grid=(batch, A_HEADS, seq // rng),
        in_specs=specs,
        out_specs=pl.BlockSpec((None, rng, HEAD_DIM), lambda b, h, i: (b, i, h)),
        out_shape=jax.ShapeDtypeStruct((batch, seq, A_GROUP_WIDTH), BF16),
        scratch_shapes=[pltpu.VMEM((rng, HEAD_DIM), BF16)]
                       + [pltpu.VMEM((rng, HEAD_DIM), F32)] * 6 + kv_scratch,
        compiler_params=_params(("arbitrary", "arbitrary", "arbitrary")),
        name="dilated_attention",
    )(*([qkv_r] * len(specs)))
    return out.reshape(batch * seq, A_GROUP_WIDTH)


def _mlp_kernel(x_ref, a_ref, wo_ref, gate_t_ref, g_ref, sh_ref, sc_ref, gate_ref,
                w1_ref, w2_ref, o_ref, h_scr):
    f = pl.program_id(1)

    @pl.when(f == 0)
    def _():
        tm = x_ref.shape[0]
        sub = min(tm, PROJ_ROW_SUB)
        for rb in range(tm // sub):
            rows = slice(rb * sub, (rb + 1) * sub)
            y = jnp.dot(a_ref[rows, :], wo_ref[...], preferred_element_type=F32)
            o_ref[rows, :] = x_ref[rows, :] + gate_t_ref[...] * y
        _norm_mod_to(h_scr, o_ref, g_ref, sh_ref, sc_ref, f * 8)

    u = jnp.dot(h_scr[...], w1_ref[...], preferred_element_type=F32)
    u = jnp.square(jnp.maximum(u, 0.0)).astype(BF16)
    o_ref[...] += gate_ref[...] * jnp.dot(u, w2_ref[...], preferred_element_type=F32)


def _attn_out_mlp(x, a, w_o, o_layer, gate_t, g, shift, scale, gate, w1, w2, layer, *, seq):
    t, d = x.shape
    k = a.shape[1]
    ff = w1.shape[2]
    tm = _tile(seq, 1024)
    tf = _tile(ff, 512)
    vec = pl.BlockSpec((None, 1, d), lambda i, f: ((i * tm) // seq, 0, 0))
    once = dict(pipeline_mode=pl.Buffered(1))
    return pl.pallas_call(
        _mlp_kernel,
        grid=(t // tm, ff // tf),
        in_specs=[
            pl.BlockSpec((tm, d), lambda i, f: (i, 0), **once),
            pl.BlockSpec((tm, k), lambda i, f: (i, 0), **once),
            pl.BlockSpec((None, k, d), lambda i, f: (o_layer, 0, 0), **once),
            vec,
            pl.BlockSpec((1, d), lambda i, f: (0, 0)), vec, vec, vec,
            pl.BlockSpec((None, d, tf), lambda i, f: (layer, 0, f)),
            pl.BlockSpec((None, tf, d), lambda i, f: (layer, f, 0)),
        ],
        out_specs=pl.BlockSpec((tm, d), lambda i, f: (i, 0)),
        out_shape=jax.ShapeDtypeStruct((t, d), F32),
        scratch_shapes=[pltpu.VMEM((tm, d), BF16)],
        compiler_params=_params(("arbitrary", "arbitrary")),
        name="attn_out_mlp",
    )(x, a, w_o, gate_t, g, shift, scale, gate, w1, w2)


def _diff_kernel(q_ref, k_ref, v_ref, lam_ref, g_ref, o_ref,
                 acc_scr, m_scr, l_scr, s_a, s_b, mx_a, mx_b, *, tq, lam_init):
    qi = pl.program_id(2)
    acc_scr[...] = jnp.zeros_like(acc_scr)
    m_scr[...] = jnp.full_like(m_scr, NEG_INF)
    l_scr[...] = jnp.zeros_like(l_scr)
    row = lax.broadcasted_iota(jnp.int32, (tq, tq), 0)
    col = lax.broadcasted_iota(jnp.int32, (tq, tq), 1)
    causal = col <= row

    def scores(j, s_ref, mx_ref, masked=False):
        k0 = pl.multiple_of(j * tq, tq)
        for c in range(2):
            cols = slice(c * HEAD_DIM, (c + 1) * HEAD_DIM)
            s = lax.dot_general(q_ref[:, cols], k_ref[pl.ds(k0, tq), cols],
                                (((1,), (1,)), ((), ())), preferred_element_type=F32)
            if masked:
                s = jnp.where(causal, s, NEG_INF)
            s_ref[c] = s
            mx_ref[c] = jnp.broadcast_to(jnp.max(s, axis=-1, keepdims=True), (tq, LANES))

    def update(j, s_ref, mx_ref):
        k0 = pl.multiple_of(j * tq, tq)
        vblk = v_ref[pl.ds(k0, tq), :]
        for c in range(2):
            m_prev = m_scr[c]
            m_new = jnp.maximum(m_prev, mx_ref[c])
            alpha = jnp.exp2(m_prev - m_new)
            p = jnp.exp2(s_ref[c] - jnp.tile(m_new, (1, tq // LANES)))
            row_sum = jnp.broadcast_to(jnp.sum(p, axis=-1, keepdims=True), (tq, LANES))
            l_scr[c] = alpha * l_scr[c] + row_sum
            m_scr[c] = m_new
            acc_scr[c] = (jnp.tile(alpha, (1, B_V_DIM // LANES)) * acc_scr[c]
                          + jnp.dot(p.astype(BF16), vblk, preferred_element_type=F32))

    @pl.when(qi > 0)
    def _():
        scores(0, s_a, mx_a)

    def pair(t, carry):
        j = 2 * t
        scores(j + 1, s_b, mx_b)
        update(j, s_a, mx_a)
        scores(j + 2, s_a, mx_a)
        update(j + 1, s_b, mx_b)
        return carry

    lax.fori_loop(0, jnp.maximum(qi - 1, 0) // 2, pair, 0)

    @pl.when(qi == 0)
    def _():
        scores(qi, s_a, mx_a, masked=True)
        update(qi, s_a, mx_a)

    @pl.when(qi % 2 == 1)
    def _():
        scores(qi, s_b, mx_b, masked=True)
        update(qi - 1, s_a, mx_a)
        update(qi, s_b, mx_b)

    @pl.when(jnp.logical_and(qi > 0, qi % 2 == 0))
    def _():
        scores(qi - 1, s_b, mx_b)
        update(qi - 2, s_a, mx_a)
        scores(qi, s_a, mx_a, masked=True)
        update(qi - 1, s_b, mx_b)
        update(qi, s_a, mx_a)

    lp = lam_ref[...]
    lam = (jnp.exp(jnp.sum(lp[0:1] * lp[1:2], axis=-1, keepdims=True))
           - jnp.exp(jnp.sum(lp[2:3] * lp[3:4], axis=-1, keepdims=True)) + lam_init)
    reps = (1, B_V_DIM // LANES)
    o = acc_scr[0] / jnp.tile(l_scr[0], reps) - lam * (acc_scr[1] / jnp.tile(l_scr[1], reps))
    ms = jnp.mean(o * o, axis=-1, keepdims=True)
    o = o * lax.rsqrt(ms + NORM_EPS) * g_ref[...] * (1.0 - lam_init)
    o_ref[...] = o.astype(o_ref.dtype)


def _diff_attention(q, kv, lam_params, subln_g, lam_init, *, batch, seq):
    t, qw = q.shape
    heads = qw // B_V_DIM
    tq = _tile(seq, 512)
    q_r = q.reshape(batch, seq, qw)
    kv_r = kv.reshape(batch, seq, 2 * qw)
    out = pl.pallas_call(
        functools.partial(_diff_kernel, tq=tq, lam_init=lam_init),
        grid=(batch, heads, seq // tq),
        in_specs=[
            pl.BlockSpec((None, tq, B_V_DIM), lambda b, h, i: (b, i, h)),
            pl.BlockSpec((None, seq, B_V_DIM), lambda b, h, i: (b, 0, h)),
            pl.BlockSpec((None, seq, B_V_DIM), lambda b, h, i: (b, 0, heads + h)),
            pl.BlockSpec(lam_params.shape, lambda b, h, i: (0, 0)),
            pl.BlockSpec((1, B_V_DIM), lambda b, h, i: (0, 0)),
        ],
        out_specs=pl.BlockSpec((None, tq, B_V_DIM), lambda b, h, i: (b, i, h)),
        out_shape=jax.ShapeDtypeStruct((batch, seq, qw), BF16),
        scratch_shapes=[pltpu.VMEM((2, tq, B_V_DIM), F32),
                        pltpu.VMEM((2, tq, LANES), F32), pltpu.VMEM((2, tq, LANES), F32),
                        pltpu.VMEM((2, tq, tq), F32), pltpu.VMEM((2, tq, tq), F32),
                        pltpu.VMEM((2, tq, LANES), F32), pltpu.VMEM((2, tq, LANES), F32)],
        compiler_params=_params(("arbitrary", "arbitrary", "arbitrary")),
        name="diff_attention",
    )(q_r, kv_r, kv_r, lam_params.astype(F32), subln_g.reshape(1, B_V_DIM))
    return out.reshape(t, qw)


def _rope_tables(positions):
    inv_freq = ROPE_THETA ** (-jnp.arange(0, ROT_DIM, 2, dtype=F32) / ROT_DIM)
    ang = positions.astype(F32).reshape(-1, 1) * inv_freq
    cos, sin = jnp.cos(ang), jnp.sin(ang)
    rest = HEAD_DIM - ROT_DIM
    cosf = jnp.concatenate([cos, cos, jnp.ones((ang.shape[0], rest), F32)], axis=-1)
    sinf = jnp.concatenate([-sin, sin, jnp.zeros((ang.shape[0], rest), F32)], axis=-1)
    return cosf, sinf


def kernel(x, c, positions, ada_w, ada_b, norm_g, a_w_qkv, a_q_gain, a_k_gain, a_w_o,
           kv_ada_w, kv_ada_b, kv_norm_g, kv_w_k, kv_w_v, kv_k_gain, b_w_q, b_q_gain,
           b_lambda, b_subln_g, b_w_o, mlp_w1, mlp_w2):
    batch, seq, d = x.shape
    depth = ada_w.shape[0]
    n_a = a_w_qkv.shape[0]
    t = batch * seq
    qk_scale = HEAD_DIM ** -0.5 * LOG2E

    cosf, sinf = _rope_tables(positions)
    c_pad = jnp.pad(c, ((0, 8 - batch % 8 if batch % 8 else 0), (0, 0)))
    mod = _modulation(c_pad, ada_w, ada_b)[:, :batch]
    kv_mod = _modulation(c_pad, kv_ada_w[None], kv_ada_b[None])[0, :batch]

    def vecs(m, n):
        return [v.reshape(batch, 1, d) for v in jnp.split(m, n, axis=-1)]

    w_qkv, w_ao = a_w_qkv.astype(BF16), a_w_o.astype(BF16)
    w_bq, w_bo = b_w_q.astype(BF16), b_w_o.astype(BF16)
    w1, w2 = mlp_w1.astype(BF16), mlp_w2.astype(BF16)
    xf = x.reshape(t, d)
    kv = None
    for layer in range(depth):
        if layer == n_a:
            shift, scale = vecs(kv_mod, 2)
            w_kv = jnp.concatenate([kv_w_k, kv_w_v], axis=1).astype(BF16)[None]
            n_kv = w_kv.shape[2]
            tn = _tile(n_kv // 2, 1024)
            k_gain_row = jnp.tile(kv_k_gain.reshape(-1), tn // (2 * HEAD_DIM))
            gains = jnp.broadcast_to(k_gain_row, (n_kv // tn, 1, tn))
            kv = _norm_mod_matmul(xf, kv_norm_g.reshape(1, d), shift, scale, w_kv, 0, gains,
                                  cosf, sinf, seq=seq, period=n_kv // tn, n_on=n_kv // (2 * tn),
                                  out_dtype=BF16)
        shift_t, scale_t, gate_t, shift_m, scale_m, gate_m = vecs(mod[layer], N_MOD)
        g_t = norm_g[layer, 0].reshape(1, d)
        if layer < n_a:
            n_groups = len(A_PATTERNS)
            ones = jnp.ones((HEAD_DIM,), F32)
            gains = jnp.stack([jnp.tile(v, A_HEADS) for g in range(n_groups)
                               for v in (a_q_gain[layer, g] * qk_scale, a_k_gain[layer, g], ones)])
            qkv = _norm_mod_matmul(xf, g_t, shift_t, scale_t, w_qkv, layer,
                                   gains.reshape(3 * n_groups, 1, A_GROUP_WIDTH),
                                   cosf, sinf, seq=seq, period=3, n_on=2, out_dtype=F32)
            mixed = _dilated_attention(qkv, [dil for _, dil in A_PATTERNS], batch, seq)
            attn, w_o, o_layer = mixed, w_ao, layer
        else:
            j = layer - n_a
            lam_init = 0.8 - 0.6 * math.exp(-0.3 * layer)
            n_q = w_bq.shape[2]
            tn = _tile(n_q, 1024)
            q_gain_row = jnp.tile(b_q_gain[j].reshape(-1) * qk_scale, tn // (2 * HEAD_DIM))
            gains = jnp.broadcast_to(q_gain_row, (n_q // tn, 1, tn))
            q = _norm_mod_matmul(xf, g_t, shift_t, scale_t, w_bq, j, gains, cosf, sinf,
                                 seq=seq, period=1, n_on=1, out_dtype=BF16)
            attn = _diff_attention(q, kv, b_lambda[j], b_subln_g[j], lam_init, batch=batch, seq=seq)
            w_o, o_layer = w_bo, j
        xf = _attn_out_mlp(xf, attn, w_o, o_layer, gate_t, norm_g[layer, 1].reshape(1, d),
                           shift_m, scale_m, gate_m, w1, w2, layer, seq=seq)
    return xf.reshape(batch, seq, d)
```

```python
import functools
import math

import jax
import jax.numpy as jnp
from jax import lax
from jax.experimental import pallas as pl
from jax.experimental.pallas import tpu as pltpu

F32 = jnp.float32
BF16 = jnp.bfloat16

HEAD_DIM = 128
ROT_DIM = HEAD_DIM // 4
ROPE_THETA = 500000.0
NORM_EPS = 1e-6
NEG_INF = -1e30
A_PATTERNS = ((128, 1), (512, 4), (2048, 16))
A_HEADS = 8
A_GROUP_WIDTH = A_HEADS * HEAD_DIM
BAND_BLOCK = 128
B_HEADS = 8
B_V_DIM = 2 * HEAD_DIM
N_MOD = 6

V7X_VMEM_LIMIT_BYTES = 56 * 1024 * 1024
LANES = 128
PROJ_ROW_SUB = 256
NORM_ROW_SUB = 128
DIL_RANGE = 2048
DIL_UNROLL = 16
LOG2E = math.log2(math.e)


def _params(semantics):
    return pltpu.CompilerParams(dimension_semantics=semantics,
                                vmem_limit_bytes=V7X_VMEM_LIMIT_BYTES)


def _tile(n, target):
    t = min(n, target)
    while n % t:
        t //= 2
    return t


def _mod_kernel(c_ref, w_ref, b_ref, o_ref):
    c = c_ref[...]
    c_act = (c * jax.nn.sigmoid(c)).astype(BF16)
    o_ref[...] = jnp.dot(c_act, w_ref[...].astype(BF16),
                         preferred_element_type=F32) + b_ref[...]


def _modulation(c_pad, w, b):
    n_layers, d, n = w.shape
    rows = c_pad.shape[0]
    tn = _tile(n, 1024)
    return pl.pallas_call(
        _mod_kernel,
        grid=(n_layers, n // tn),
        in_specs=[
            pl.BlockSpec((rows, d), lambda l, j: (0, 0)),
            pl.BlockSpec((None, d, tn), lambda l, j: (l, 0, j)),
            pl.BlockSpec((None, 1, tn), lambda l, j: (l, 0, j)),
        ],
        out_specs=pl.BlockSpec((None, rows, tn), lambda l, j: (l, 0, j)),
        out_shape=jax.ShapeDtypeStruct((n_layers, rows, n), F32),
        compiler_params=_params(("arbitrary", "arbitrary")),
        name="adaln_modulation",
    )(c_pad, w, b.reshape(n_layers, 1, n))


def _norm_mod_to(h_scr, x_ref, g_ref, sh_ref, sc_ref, zero):
    shift = sh_ref[...]
    gain = g_ref[...] * (1.0 + sc_ref[...])
    tm = x_ref.shape[0]
    sub = min(tm, NORM_ROW_SUB)
    rinv = []
    for rb in range(tm // sub):
        x = x_ref[rb * sub:(rb + 1) * sub, :]
        rinv.append(lax.rsqrt(jnp.mean(x * x, axis=-1, keepdims=True) + NORM_EPS))
    for rb in range(tm // sub):
        x = x_ref[pl.ds(pl.multiple_of(zero + rb * sub, 8), sub), :]
        h_scr[rb * sub:(rb + 1) * sub, :] = (x * rinv[rb] * gain + shift).astype(BF16)


def _qk_norm_rope(a, gain, cosf, sinf):
    half = ROT_DIM // 2
    ms = jnp.mean(a * a, axis=-1, keepdims=True)
    y = a * lax.rsqrt(ms + NORM_EPS) * gain
    lane = lax.broadcasted_iota(jnp.int32, y.shape, 1)
    partner = jnp.where(lane < half,
                        pltpu.roll(y, HEAD_DIM - half, 1),
                        pltpu.roll(y, half, 1))
    return y * cosf + partner * sinf


def _nmm_kernel(x_ref, g_ref, sh_ref, sc_ref, w_ref, gain_ref, cos_ref, sin_ref,
                o_ref, h_scr, *, period, n_on):
    j = pl.program_id(1)

    @pl.when(j == 0)
    def _():
        _norm_mod_to(h_scr, x_ref, g_ref, sh_ref, sc_ref, j * 8)

    qk_tile = lax.rem(j, period) < n_on
    tm, tn = o_ref.shape
    sub = min(tm, PROJ_ROW_SUB)

    @pl.when(qk_tile)
    def _():
        for rb in range(tm // sub):
            rows = slice(rb * sub, (rb + 1) * sub)
            acc = jnp.dot(h_scr[rows, :], w_ref[...], preferred_element_type=F32)
            cosf = cos_ref[rows, :]
            sinf = sin_ref[rows, :]
            for c in range(tn // HEAD_DIM):
                cols = slice(c * HEAD_DIM, (c + 1) * HEAD_DIM)
                o_ref[rows, cols] = _qk_norm_rope(acc[:, cols], gain_ref[:, cols],
                                                  cosf, sinf).astype(o_ref.dtype)

    @pl.when(jnp.logical_not(qk_tile))
    def _():
        for rb in range(tm // sub):
            rows = slice(rb * sub, (rb + 1) * sub)
            o_ref[rows, :] = jnp.dot(h_scr[rows, :], w_ref[...],
                                     preferred_element_type=F32).astype(o_ref.dtype)


def _norm_mod_matmul(x, g, shift, scale, w, layer, gains, cosf, sinf, *, seq, period, n_on,
                     out_dtype):
    t, d = x.shape
    n = w.shape[2]
    tm = _tile(seq, 1024)
    tn = gains.shape[2]
    kern = functools.partial(_nmm_kernel, period=period, n_on=n_on)
    return pl.pallas_call(
        kern,
        grid=(t // tm, n // tn),
        in_specs=[
            pl.BlockSpec((tm, d), lambda i, j: (i, 0)),
            pl.BlockSpec((1, d), lambda i, j: (0, 0)),
            pl.BlockSpec((None, 1, d), lambda i, j: ((i * tm) // seq, 0, 0)),
            pl.BlockSpec((None, 1, d), lambda i, j: ((i * tm) // seq, 0, 0)),
            pl.BlockSpec((None, d, tn), lambda i, j: (layer, 0, j)),
            pl.BlockSpec((None, 1, tn), lambda i, j: (j, 0, 0)),
            pl.BlockSpec((tm, LANES), lambda i, j: (i, 0)),
            pl.BlockSpec((tm, LANES), lambda i, j: (i, 0)),
        ],
        out_specs=pl.BlockSpec((tm, tn), lambda i, j: (i, j)),
        out_shape=jax.ShapeDtypeStruct((t, n), out_dtype),
        scratch_shapes=[pltpu.VMEM((tm, d), BF16)],
        compiler_params=_params(("arbitrary", "arbitrary")),
        name="norm_mod_proj",
    )(x, g, shift, scale, w, gains, cosf, sinf)


def _dil_attn_kernel(*refs, rng, dilations):
    n_g = len(dilations)
    ins, o_ref = refs[:3 * n_g], refs[3 * n_g]
    scratch = refs[3 * n_g + 1:]
    qd = scratch[0]
    o_w = [(scratch[1 + 2 * g], scratch[2 + 2 * g]) for g in range(n_g)]
    kv_scr = scratch[7:]
    blk = BAND_BLOCK
    first_range = pl.program_id(2) == 0

    qi = lax.broadcasted_iota(jnp.int32, (blk, 2 * blk), 0)
    ki = lax.broadcasted_iota(jnp.int32, (blk, 2 * blk), 1)
    band = jnp.where((ki >= qi) & (ki <= qi + blk), 0.0, NEG_INF).astype(F32)
    no_prev = jnp.where(ki < blk, NEG_INF, 0.0).astype(F32)

    for g, d in enumerate(dilations):
        q_ref, k_ref, v_ref = ins[3 * g:3 * g + 3]
        kd, vd = kv_scr[2 * g:2 * g + 2]
        lq = rng // d
        res_rows = blk + lq
        per_res = lq // blk

        @pl.when(first_range)
        def _(kd=kd, vd=vd, d=d, res_rows=res_rows):
            for r in range(d):
                kd[r * res_rows:r * res_rows + blk, :] = jnp.zeros((blk, HEAD_DIM), BF16)
                vd[r * res_rows:r * res_rows + blk, :] = jnp.zeros((blk, HEAD_DIM), BF16)

        def take(ref, r, n, d=d):
            rows = ref[...] if d == 1 else ref[pl.ds(r, n, stride=d), :]
            return rows.astype(BF16)

        for r in range(d):
            qd[r * lq:(r + 1) * lq, :] = take(q_ref, r, lq)
            kd[r * res_rows + blk:(r + 1) * res_rows, :] = take(k_ref, r, lq)
            vd[r * res_rows + blk:(r + 1) * res_rows, :] = take(v_ref, r, lq)

        o_dst, w_dst = o_w[g]

        def unit(u, carry, d=d, per_res=per_res, res_rows=res_rows, kd=kd, vd=vd,
                 o_dst=o_dst, w_dst=w_dst):
            shift = per_res.bit_length() - 1
            r = lax.shift_right_logical(u, shift)
            i = lax.bitwise_and(u, per_res - 1)
            q = qd[pl.ds(pl.multiple_of(u * blk, blk), blk), :]
            k0 = pl.multiple_of(r * res_rows + i * blk, blk)
            kw = kd[pl.ds(k0, 2 * blk), :]
            vw = vd[pl.ds(k0, 2 * blk), :]
            first = jnp.logical_and(first_range, i == 0).astype(F32)
            s = lax.dot_general(q, kw, (((1,), (1,)), ((), ())),
                                preferred_element_type=F32) + (band + first * no_prev)
            m = jnp.max(s, axis=-1, keepdims=True)
            p = jnp.exp2(s - m)
            den = jnp.sum(p, axis=-1, keepdims=True)
            acc = jnp.dot(p.astype(BF16), vw, preferred_element_type=F32)
            if d == 1:
                dst = pl.ds(pl.multiple_of(u * blk, blk), blk)
            else:
                dst = pl.ds(r + i * (blk * d), blk, stride=d)
            o_dst[dst, :] = acc * (1.0 / den)
            w_dst[dst, :] = jnp.broadcast_to(m + jnp.log2(den), (blk, HEAD_DIM))
            return carry

        lax.fori_loop(0, rng // blk, unit, 0, unroll=DIL_UNROLL)

        for r in range(d):
            kd[r * res_rows:r * res_rows + blk, :] = kd[(r + 1) * res_rows - blk:(r + 1) * res_rows, :]
            vd[r * res_rows:r * res_rows + blk, :] = vd[(r + 1) * res_rows - blk:(r + 1) * res_rows, :]

    ws = [w[...] for _, w in o_w]
    w_max = functools.reduce(jnp.maximum, ws)
    es = [jnp.exp2(w - w_max) for w in ws]
    num = sum(e * o[...] for e, (o, _) in zip(es, o_w))
    o_ref[...] = (num / sum(es)).astype(o_ref.dtype)


def _dilated_attention(qkv, dilations, batch, seq):
    width = qkv.shape[1]
    rng = DIL_RANGE
    blk = BAND_BLOCK
    assert seq % rng == 0 and all(rng % (d * blk) == 0 for d in dilations)
    qkv_r = qkv.reshape(batch, seq, width)

    def spec(g, which):
        return pl.BlockSpec((None, rng, HEAD_DIM),
                            lambda b, h, i: (b, i, (g * 3 + which) * A_HEADS + h))

    specs = [spec(g, which) for g in range(len(dilations)) for which in range(3)]
    kv_scratch = [pltpu.VMEM((rng + blk * d, HEAD_DIM), BF16) for d in dilations for _ in range(2)]
    out = pl.pallas_call(
        functools.partial(_dil_attn_kernel, rng=rng, dilations=tuple(dilations)),
        grid=(batch, A_HEADS, seq // rng),
        in_specs=specs,
        out_specs=pl.BlockSpec((None, rng, HEAD_DIM), lambda b, h, i: (b, i, h)),
        out_shape=jax.ShapeDtypeStruct((batch, seq, A_GROUP_WIDTH), BF16),
        scratch_shapes=[pltpu.VMEM((rng, HEAD_DIM), BF16)]
                       + [pltpu.VMEM((rng, HEAD_DIM), F32)] * 6 + kv_scratch,
        compiler_params=_params(("arbitrary", "arbitrary", "arbitrary")),
        name="dilated_attention",
    )(*([qkv_r] * len(specs)))
    return out.reshape(batch * seq, A_GROUP_WIDTH)


def _mlp_kernel(x_ref, a_ref, wo_ref, gate_t_ref, g_ref, sh_ref, sc_ref, gate_ref,
                w1_ref, w2_ref, o_ref, h_scr):
    f = pl.program_id(1)

    @pl.when(f == 0)
    def _():
        tm = x_ref.shape[0]
        sub = min(tm, PROJ_ROW_SUB)
        for rb in range(tm // sub):
            rows = slice(rb * sub, (rb + 1) * sub)
            y = jnp.dot(a_ref[rows, :], wo_ref[...], preferred_element_type=F32)
            o_ref[rows, :] = x_ref[rows, :] + gate_t_ref[...] * y
        _norm_mod_to(h_scr, o_ref, g_ref, sh_ref, sc_ref, f * 8)

    u = jnp.dot(h_scr[...], w1_ref[...], preferred_element_type=F32)
    u = jnp.square(jnp.maximum(u, 0.0)).astype(BF16)
    o_ref[...] += gate_ref[...] * jnp.dot(u, w2_ref[...], preferred_element_type=F32)


def _attn_out_mlp(x, a, w_o, o_layer, gate_t, g, shift, scale, gate, w1, w2, layer, *, seq):
    t, d = x.shape
    k = a.shape[1]
    ff = w1.shape[2]
    tm = _tile(seq, 1024)
    tf = _tile(ff, 512)
    vec = pl.BlockSpec((None, 1, d), lambda i, f: ((i * tm) // seq, 0, 0))
    once = dict(pipeline_mode=pl.Buffered(1))
    return pl.pallas_call(
        _mlp_kernel,
        grid=(t // tm, ff // tf),
        in_specs=[
            pl.BlockSpec((tm, d), lambda i, f: (i, 0), **once),
            pl.BlockSpec((tm, k), lambda i, f: (i, 0), **once),
            pl.BlockSpec((None, k, d), lambda i, f: (o_layer, 0, 0), **once),
            vec,
            pl.BlockSpec((1, d), lambda i, f: (0, 0)), vec, vec, vec,
            pl.BlockSpec((None, d, tf), lambda i, f: (layer, 0, f)),
            pl.BlockSpec((None, tf, d), lambda i, f: (layer, f, 0)),
        ],
        out_specs=pl.BlockSpec((tm, d), lambda i, f: (i, 0)),
        out_shape=jax.ShapeDtypeStruct((t, d), F32),
        scratch_shapes=[pltpu.VMEM((tm, d), BF16)],
        compiler_params=_params(("arbitrary", "arbitrary")),
        name="attn_out_mlp",
    )(x, a, w_o, gate_t, g, shift, scale, gate, w1, w2)


def _diff_kernel(q_ref, k_ref, v_ref, lam_ref, g_ref, o_ref,
                 acc_scr, m_scr, l_scr, s_a, s_b, mx_a, mx_b, *, tq, lam_init):
    qi = pl.program_id(2)
    acc_scr[...] = jnp.zeros_like(acc_scr)
    m_scr[...] = jnp.full_like(m_scr, NEG_INF)
    l_scr[...] = jnp.zeros_like(l_scr)
    row = lax.broadcasted_iota(jnp.int32, (tq, tq), 0)
    col = lax.broadcasted_iota(jnp.int32, (tq, tq), 1)
    causal = col <= row

    def scores(j, s_ref, mx_ref, masked=False):
        k0 = pl.multiple_of(j * tq, tq)
        for c in range(2):
            cols = slice(c * HEAD_DIM, (c + 1) * HEAD_DIM)
            s = lax.dot_general(q_ref[:, cols], k_ref[pl.ds(k0, tq), cols],
                                (((1,), (1,)), ((), ())), preferred_element_type=F32)
            if masked:
                s = jnp.where(causal, s, NEG_INF)
            s_ref[c] = s
            mx_ref[c] = jnp.broadcast_to(jnp.max(s, axis=-1, keepdims=True), (tq, LANES))

    def update(j, s_ref, mx_ref):
        k0 = pl.multiple_of(j * tq, tq)
        vblk = v_ref[pl.ds(k0, tq), :]
        for c in range(2):
            m_prev = m_scr[c]
            m_new = jnp.maximum(m_prev, mx_ref[c])
            alpha = jnp.exp2(m_prev - m_new)
            p = jnp.exp2(s_ref[c] - jnp.tile(m_new, (1, tq // LANES)))
            row_sum = jnp.broadcast_to(jnp.sum(p, axis=-1, keepdims=True), (tq, LANES))
            l_scr[c] = alpha * l_scr[c] + row_sum
            m_scr[c] = m_new
            acc_scr[c] = (jnp.tile(alpha, (1, B_V_DIM // LANES)) * acc_scr[c]
                          + jnp.dot(p.astype(BF16), vblk, preferred_element_type=F32))

    @pl.when(qi > 0)
    def _():
        scores(0, s_a, mx_a)

    def pair(t, carry):
        j = 2 * t
        scores(j + 1, s_b, mx_b)
        update(j, s_a, mx_a)
        scores(j + 2, s_a, mx_a)
        update(j + 1, s_b, mx_b)
        return carry

    lax.fori_loop(0, jnp.maximum(qi - 1, 0) // 2, pair, 0)

    @pl.when(qi == 0)
    def _():
        scores(qi, s_a, mx_a, masked=True)
        update(qi, s_a, mx_a)

    @pl.when(qi % 2 == 1)
    def _():
        scores(qi, s_b, mx_b, masked=True)
        update(qi - 1, s_a, mx_a)
        update(qi, s_b, mx_b)

    @pl.when(jnp.logical_and(qi > 0, qi % 2 == 0))
    def _():
        scores(qi - 1, s_b, mx_b)
        update(qi - 2, s_a, mx_a)
        scores(qi, s_a, mx_a, masked=True)
        update(qi - 1, s_b, mx_b)
        update(qi, s_a, mx_a)

    lp = lam_ref[...]
    lam = (jnp.exp(jnp.sum(lp[0:1] * lp[1:2], axis=-1, keepdims=True))
           - jnp.exp(jnp.sum(lp[2:3] * lp[3:4], axis=-1, keepdims=True)) + lam_init)
    reps = (1, B_V_DIM // LANES)
    o = acc_scr[0] / jnp.tile(l_scr[0], reps) - lam * (acc_scr[1] / jnp.tile(l_scr[1], reps))
    ms = jnp.mean(o * o, axis=-1, keepdims=True)
    o = o * lax.rsqrt(ms + NORM_EPS) * g_ref[...] * (1.0 - lam_init)
    o_ref[...] = o.astype(o_ref.dtype)


def _diff_attention(q, kv, lam_params, subln_g, lam_init, *, batch, seq):
    t, qw = q.shape
    heads = qw // B_V_DIM
    tq = _tile(seq, 512)
    q_r = q.reshape(batch, seq, qw)
    kv_r = kv.reshape(batch, seq, 2 * qw)
    out = pl.pallas_call(
        functools.partial(_diff_kernel, tq=tq, lam_init=lam_init),
        grid=(batch, heads, seq // tq),
        in_specs=[
            pl.BlockSpec((None, tq, B_V_DIM), lambda b, h, i: (b, i, h)),
            pl.BlockSpec((None, seq, B_V_DIM), lambda b, h, i: (b, 0, h)),
            pl.BlockSpec((None, seq, B_V_DIM), lambda b, h, i: (b, 0, heads + h)),
            pl.BlockSpec(lam_params.shape, lambda b, h, i: (0, 0)),
            pl.BlockSpec((1, B_V_DIM), lambda b, h, i: (0, 0)),
        ],
        out_specs=pl.BlockSpec((None, tq, B_V_DIM), lambda b, h, i: (b, i, h)),
        out_shape=jax.ShapeDtypeStruct((batch, seq, qw), BF16),
        scratch_shapes=[pltpu.VMEM((2, tq, B_V_DIM), F32),
                        pltpu.VMEM((2, tq, LANES), F32), pltpu.VMEM((2, tq, LANES), F32),
                        pltpu.VMEM((2, tq, tq), F32), pltpu.VMEM((2, tq, tq), F32),
                        pltpu.VMEM((2, tq, LANES), F32), pltpu.VMEM((2, tq, LANES), F32)],
        compiler_params=_params(("arbitrary", "arbitrary", "arbitrary")),
        name="diff_attention",
    )(q_r, kv_r, kv_r, lam_params.astype(F32), subln_g.reshape(1, B_V_DIM))
    return out.reshape(t, qw)


def _rope_tables(positions):
    inv_freq = ROPE_THETA ** (-jnp.arange(0, ROT_DIM, 2, dtype=F32) / ROT_DIM)
    ang = positions.astype(F32).reshape(-1, 1) * inv_freq
    cos, sin = jnp.cos(ang), jnp.sin(ang)
    rest = HEAD_DIM - ROT_DIM
    cosf = jnp.concatenate([cos, cos, jnp.ones((ang.shape[0], rest), F32)], axis=-1)
    sinf = jnp.concatenate([-sin, sin, jnp.zeros((ang.shape[0], rest), F32)], axis=-1)
    return cosf, sinf


def kernel(x, c, positions, ada_w, ada_b, norm_g, a_w_qkv, a_q_gain, a_k_gain, a_w_o,
           kv_ada_w, kv_ada_b, kv_norm_g, kv_w_k, kv_w_v, kv_k_gain, b_w_q, b_q_gain,
           b_lambda, b_subln_g, b_w_o, mlp_w1, mlp_w2):
    batch, seq, d = x.shape
    depth = ada_w.shape[0]
    n_a = a_w_qkv.shape[0]
    t = batch * seq
    qk_scale = HEAD_DIM ** -0.5 * LOG2E

    cosf, sinf = _rope_tables(positions)
    c_pad = jnp.pad(c, ((0, 8 - batch % 8 if batch % 8 else 0), (0, 0)))
    mod = _modulation(c_pad, ada_w, ada_b)[:, :batch]
    kv_mod = _modulation(c_pad, kv_ada_w[None], kv_ada_b[None])[0, :batch]

    def vecs(m, n):
        return [v.reshape(batch, 1, d) for v in jnp.split(m, n, axis=-1)]

    w_qkv, w_ao = a_w_qkv.astype(BF16), a_w_o.astype(BF16)
    w_bq, w_bo = b_w_q.astype(BF16), b_w_o.astype(BF16)
    w1, w2 = mlp_w1.astype(BF16), mlp_w2.astype(BF16)
    xf = x.reshape(t, d)
    kv = None
    for layer in range(depth):
        if layer == n_a:
            shift, scale = vecs(kv_mod, 2)
            w_kv = jnp.concatenate([kv_w_k, kv_w_v], axis=1).astype(BF16)[None]
            n_kv = w_kv.shape[2]
            tn = _tile(n_kv // 2, 1024)
            k_gain_row = jnp.tile(kv_k_gain.reshape(-1), tn // (2 * HEAD_DIM))
            gains = jnp.broadcast_to(k_gain_row, (n_kv // tn, 1, tn))
            kv = _norm_mod_matmul(xf, kv_norm_g.reshape(1, d), shift, scale, w_kv, 0, gains,
                                  cosf, sinf, seq=seq, period=n_kv // tn, n_on=n_kv // (2 * tn),
                                  out_dtype=BF16)
        shift_t, scale_t, gate_t, shift_m, scale_m, gate_m = vecs(mod[layer], N_MOD)
        g_t = norm_g[layer, 0].reshape(1, d)
        if layer < n_a:
            n_groups = len(A_PATTERNS)
            ones = jnp.ones((HEAD_DIM,), F32)
            gains = jnp.stack([jnp.tile(v, A_HEADS) for g in range(n_groups)
                               for v in (a_q_gain[layer, g] * qk_scale, a_k_gain[layer, g], ones)])
            qkv = _norm_mod_matmul(xf, g_t, shift_t, scale_t, w_qkv, layer,
                                   gains.reshape(3 * n_groups, 1, A_GROUP_WIDTH),
                                   cosf, sinf, seq=seq, period=3, n_on=2, out_dtype=F32)
            mixed = _dilated_attention(qkv, [dil for _, dil in A_PATTERNS], batch, seq)
            attn, w_o, o_layer = mixed, w_ao, layer
        else:
            j = layer - n_a
            lam_init = 0.8 - 0.6 * math.exp(-0.3 * layer)
            n_q = w_bq.shape[2]
            tn = _tile(n_q, 1024)
            q_gain_row = jnp.tile(b_q_gain[j].reshape(-1) * qk_scale, tn // (2 * HEAD_DIM))
            gains = jnp.broadcast_to(q_gain_row, (n_q // tn, 1, tn))
            q = _norm_mod_matmul(xf, g_t, shift_t, scale_t, w_bq, j, gains, cosf, sinf,
                                 seq=seq, period=1, n_on=1, out_dtype=BF16)
            attn = _diff_attention(q, kv, b_lambda[j], b_subln_g[j], lam_init, batch=batch, seq=seq)
            w_o, o_layer = w_bo, j
        xf = _attn_out_mlp(xf, attn, w_o, o_layer, gate_t, norm_g[layer, 1].reshape(1, d),
                           shift_m, scale_m, gate_m, w1, w2, layer, seq=seq)
    return xf.reshape(batch, seq, d)
```
